```python
import jax
import jax.numpy as jnp
from jax import lax
import numpy as np

D_MODEL = 2048
BATCH = 4
SEQ = 2048
DEPTH = 4
DEC_BATCH = 32
DEC_SEQ = 4
PAST_LEN = 16384
PAGE_SIZE = 128

HEAD_DIM = 64
A_HEADS = (3 * D_MODEL // 8) // HEAD_DIM
A_BRANCHES = ((128, 1), (512, 4), (2048, 16))
A_WINDOW_MAX = 2048
B_HEAD_DIM = 128
B_HEADS = (D_MODEL // 4) // B_HEAD_DIM
CONV_W = 4
DELTA_CHUNK = 64
C_HEADS = (3 * D_MODEL // 8) // HEAD_DIM
C_KV_HEADS = C_HEADS // 3
C_WINDOW = 128
A_WIDTH = A_HEADS * HEAD_DIM
B_WIDTH = B_HEADS * B_HEAD_DIM
C_WIDTH = C_HEADS * HEAD_DIM
MIX_WIDTH = A_WIDTH + B_WIDTH + C_WIDTH
D_FF = 4 * D_MODEL
BLOCK = 128
EPS = 1e-6
ATTN_SCALE = HEAD_DIM ** -0.5
IN_SIZES = (3 * A_WIDTH, 3 * B_WIDTH, B_WIDTH, B_HEADS, B_HEADS, C_WIDTH, 2 * C_KV_HEADS * HEAD_DIM)
IN_COLS = sum(IN_SIZES)
IN_SPLITS = tuple(int(c) for c in np.cumsum(IN_SIZES)[:-1])

kernel_name = 'hybrid_dilated_delta_swa_step'


def alibi_slopes(n):
    return np.asarray([2.0 ** (-8.0 * (i + 1) / n) for i in range(n)], dtype=np.float32)


def rms_norm(x, g):
    xf = x.astype(jnp.float32)
    y = xf * lax.rsqrt(jnp.mean(xf * xf, axis=-1, keepdims=True) + EPS)
    return (y * g.astype(jnp.float32)).astype(x.dtype)


def l2_normalize(x):
    return x * lax.rsqrt(jnp.sum(x * x, axis=-1, keepdims=True) + EPS)


def banded_attention(q, k, v, window, step, slopes):
    n, L, hq, dh = q.shape
    hk = k.shape[2]
    grp = hq // hk
    nb = -(-L // BLOCK)
    pad = nb * BLOCK - L
    qb = jnp.pad(q, ((0, 0), (0, pad), (0, 0), (0, 0))).reshape(n, nb, BLOCK, hk, grp, dh)
    kp = jnp.pad(k, ((0, 0), (BLOCK, pad), (0, 0), (0, 0))).reshape(n, nb + 1, BLOCK, hk, dh)
    vp = jnp.pad(v, ((0, 0), (BLOCK, pad), (0, 0), (0, 0))).reshape(n, nb + 1, BLOCK, hk, dh)
    kb = jnp.concatenate([kp[:, :-1], kp[:, 1:]], axis=2)
    vb = jnp.concatenate([vp[:, :-1], vp[:, 1:]], axis=2)
    s = jnp.einsum('nbqhgd,nbkhd->nbhgqk', qb, kb, preferred_element_type=jnp.float32) * ATTN_SCALE
    qpos = np.arange(BLOCK)[:, None] + BLOCK
    kpos = np.arange(2 * BLOCK)[None, :]
    dist = qpos - kpos
    kglob = (np.arange(nb) * BLOCK - BLOCK)[:, None, None] + kpos[None]
    valid = (dist[None] >= 0) & (dist[None] <= window) & (kglob >= 0)
    bias = -(slopes.reshape(hk, grp)[:, :, None, None] * (step * dist).astype(np.float32))
    s = jnp.where(valid[None, :, None, None], s + bias, -jnp.inf)
    m = jnp.max(s, axis=-1, keepdims=True)
    p = jnp.exp(s - m)
    den = jnp.sum(p, axis=-1, keepdims=True)
    o = jnp.einsum('nbhgqk,nbkhd->nbhgqd', p, vb.astype(jnp.float32)) / den
    lse = (m + jnp.log(den))[..., 0]
    o = o.transpose(0, 1, 4, 2, 3, 5).reshape(n, nb * BLOCK, hq, dh)[:, :L]
    lse = lse.transpose(0, 1, 4, 2, 3).reshape(n, nb * BLOCK, hq)[:, :L]
    return o, lse


def attend_rows(q, kg, vg, valid, dist, slopes):
    n, T, hq, dh = q.shape
    hk = kg.shape[3]
    grp = hq // hk
    qg = q.reshape(n, T, hk, grp, dh)
    s = jnp.einsum('nthgd,ntjhd->nthgj', qg, kg, preferred_element_type=jnp.float32) * ATTN_SCALE
    bias = -(slopes.reshape(hk, grp)[None, :, :, None] * dist[:, None, None, :])
    s = jnp.where(valid[:, None, None, :], s + bias, -jnp.inf)
    m = jnp.max(s, axis=-1, keepdims=True)
    p = jnp.exp(s - m)
    den = jnp.sum(p, axis=-1, keepdims=True)
    o = jnp.einsum('nthgj,ntjhd->nthgd', p, vg.astype(jnp.float32)) / den
    lse = (m + jnp.log(den))[..., 0]
    return o.reshape(n, T, hq, dh), lse.reshape(n, T, hq)


def gather_rows(past, new, idx):
    P = past.shape[1]
    T = new.shape[1]
    from_past = idx < P
    rp = past[:, np.clip(idx, 0, P - 1)]
    rn = new[:, np.clip(idx - P, 0, T - 1)]
    sel = from_past.reshape(from_past.shape + (1,) * (rp.ndim - 3))
    return jnp.where(sel[None], rp, rn.astype(rp.dtype))


def merge_branches(outs, lses):
    wts = jax.nn.softmax(jnp.stack(lses), axis=0)
    return jnp.sum(wts[..., None] * jnp.stack(outs), axis=0)


def dilated_prompt(q, k, v):
    n, S, H, dh = q.shape
    slopes = alibi_slopes(A_HEADS)
    outs, lses = [], []
    for window, dil in A_BRANCHES:
        sub = S // dil
        fold = lambda t: t.reshape(n, sub, dil, H, dh).transpose(0, 2, 1, 3, 4).reshape(n * dil, sub, H, dh)
        o, lse = banded_attention(fold(q), fold(k), fold(v), window // dil, dil, slopes)
        outs.append(o.reshape(n, dil, sub, H, dh).transpose(0, 2, 1, 3, 4).reshape(n, S, H, dh))
        lses.append(lse.reshape(n, dil, sub, H).transpose(0, 2, 1, 3).reshape(n, S, H))
    return merge_branches(outs, lses)


def dilated_sample(q, past_kv, new_kv):
    P = past_kv.shape[1]
    T = q.shape[1]
    slopes = alibi_slopes(A_HEADS)
    qi = np.arange(T)[:, None]
    outs, lses = [], []
    for window, dil in A_BRANCHES:
        nk = window // dil + 1
        dist = np.broadcast_to(dil * np.arange(nk)[None, :], (T, nk))
        idx = P + qi - dist
        rows = gather_rows(past_kv, new_kv, idx)
        o, lse = attend_rows(q, rows[:, :, :, 0], rows[:, :, :, 1], idx >= 0, dist.astype(np.float32), slopes)
        outs.append(o)
        lses.append(lse)
    return merge_branches(outs, lses)


def swa_sample(q, past_kv, new_kv):
    P = past_kv.shape[1]
    T = q.shape[1]
    idx = np.broadcast_to(np.arange(P + T)[None, :], (T, P + T))
    dist = (P + np.arange(T)[:, None]) - idx
    valid = (dist >= 0) & (dist <= C_WINDOW)
    rows = gather_rows(past_kv, new_kv, idx)
    return attend_rows(q, rows[:, :, :, 0], rows[:, :, :, 1], valid, dist.astype(np.float32), alibi_slopes(C_HEADS))


def gated_delta_chunked(q, k, v, g, beta, s0):
    n, L, H, dk = q.shape
    dv = v.shape[-1]
    C = min(DELTA_CHUNK, L)
    nc = -(-L // C)
    pad = nc * C - L

    def prep(t):
        t = jnp.pad(t.astype(jnp.float32), [(0, 0), (0, pad)] + [(0, 0)] * (t.ndim - 2))
        t = t.reshape((n, nc, C) + t.shape[2:])
        return jnp.moveaxis(t, 3, 1)

    q, k, v, g, beta = prep(q), prep(k), prep(v), prep(g), prep(beta)
    gc = jnp.cumsum(g, axis=-1)
    tri_incl = np.tril(np.ones((C, C), dtype=bool))
    tri_strict = np.tril(np.ones((C, C), dtype=bool), -1)
    decay = jnp.exp(jnp.where(tri_incl, gc[..., :, None] - gc[..., None, :], -jnp.inf))
    kb = k * beta[..., None]
    vb = v * beta[..., None]
    a = jnp.where(tri_strict, jnp.einsum('nhcid,nhcjd->nhcij', kb, k) * decay, 0.0)
    rhs = jnp.concatenate([vb, kb * jnp.exp(gc)[..., None]], axis=-1)
    sol = lax.linalg.triangular_solve(a + np.eye(C, dtype=np.float32), rhs, left_side=True, lower=True)
    u, w = sol[..., :dv], sol[..., dv:]
    attn = jnp.einsum('nhcid,nhcjd->nhcij', q, k) * decay
    qd = q * jnp.exp(gc)[..., None]
    kd = k * jnp.exp(gc[..., -1:] - gc)[..., None]
    glast = jnp.exp(gc[..., -1])
    xs = tuple(jnp.moveaxis(t, 2, 0) for t in (qd, kd, u, w, attn, glast))

    def step(S, xc):
        qd_c, kd_c, u_c, w_c, attn_c, gl_c = xc
        v_new = u_c - jnp.einsum('nhcd,nhde->nhce', w_c, S)
        o = jnp.einsum('nhcd,nhde->nhce', qd_c, S) + jnp.einsum('nhij,nhje->nhie', attn_c, v_new)
        S = S * gl_c[..., None, None] + jnp.einsum('nhcd,nhce->nhde', kd_c, v_new)
        return S, o

    S, o = lax.scan(step, s0.astype(jnp.float32), xs)
    o = o.transpose(1, 0, 3, 2, 4).reshape(n, nc * C, H, dv)[:, :L]
    return o, S


def delta_branch(b_qkv, b_z, b_beta, b_alpha, conv_buf, s0, conv_w, a_log, dt_bias, dn_g):
    n, L, _ = b_qkv.shape
    full = jnp.concatenate([conv_buf.astype(b_qkv.dtype), b_qkv], axis=1)
    conv = sum(full[:, j:j + L] * conv_w[j] for j in range(CONV_W))
    conv = jax.nn.silu(conv).astype(jnp.float32).reshape(n, L, 3, B_HEADS, B_HEAD_DIM)
    q = l2_normalize(conv[:, :, 0]) * (B_HEAD_DIM ** -0.5)
    k = l2_normalize(conv[:, :, 1])
    v = conv[:, :, 2]
    beta = jax.nn.sigmoid(b_beta.astype(jnp.float32))
    g = -jnp.exp(a_log.astype(jnp.float32)) * jax.nn.softplus(b_alpha.astype(jnp.float32) + dt_bias.astype(jnp.float32))
    o, s_new = gated_delta_chunked(q, k, v, g, beta, s0)
    o = rms_norm(o, dn_g) * jax.nn.silu(b_z.astype(jnp.float32).reshape(n, L, B_HEADS, B_HEAD_DIM))
    return o.reshape(n, L, B_WIDTH), s_new, full[:, full.shape[1] - (CONV_W - 1):]


def token_mixers(h, dil_past, swa_past, conv_buf, s0, w_in, conv_w, a_log, dt_bias, dn_g, sinks):
    n, L, _ = h.shape
    z = jnp.einsum('nld,de->nle', h, w_in)
    a_qkv, b_qkv, b_z, b_beta, b_alpha, c_q, c_kv = jnp.split(z, IN_SPLITS, axis=-1)
    a_qkv = a_qkv.reshape(n, L, 3, A_HEADS, HEAD_DIM)
    a_q, a_kv = a_qkv[:, :, 0], a_qkv[:, :, 1:]
    c_q = c_q.reshape(n, L, C_HEADS, HEAD_DIM)
    c_kv = c_kv.reshape(n, L, 2, C_KV_HEADS, HEAD_DIM)
    if dil_past is None:
        a_out = dilated_prompt(a_q, a_kv[:, :, 0], a_kv[:, :, 1])
        c_o, c_lse = banded_attention(c_q, c_kv[:, :, 0], c_kv[:, :, 1], C_WINDOW, 1, alibi_slopes(C_HEADS))
        a_rows = a_kv[:, L - min(A_WINDOW_MAX, L):]
        c_rows = c_kv[:, L - min(C_WINDOW, L):]
    else:
        a_out = dilated_sample(a_q, dil_past, a_kv)
        c_o, c_lse = swa_sample(c_q, swa_past, c_kv)
        a_rows, c_rows = a_kv, c_kv
    c_out = c_o * jnp.exp(c_lse - jnp.logaddexp(c_lse, sinks.astype(jnp.float32)))[..., None]
    b_out, s_new, conv_new = delta_branch(b_qkv, b_z, b_beta, b_alpha, conv_buf, s0, conv_w, a_log, dt_bias, dn_g)
    mix = jnp.concatenate([a_out.reshape(n, L, A_WIDTH).astype(h.dtype), b_out.astype(h.dtype),
                           c_out.reshape(n, L, C_WIDTH).astype(h.dtype)], axis=-1)
    return mix, a_rows, c_rows, s_new.astype(s0.dtype), conv_new


def sandwich_residual(x, mix, w_out, g_post_mix, g_pre_mlp, w_up, w_down, g_post_mlp):
    x = x + rms_norm(jnp.einsum('nlm,md->nld', mix, w_out), g_post_mix)
    hm = rms_norm(x, g_pre_mlp)
    u = jnp.square(jax.nn.relu(jnp.einsum('nld,df->nlf', hm, w_up)))
    return x + rms_norm(jnp.einsum('nlf,fd->nld', u, w_down), g_post_mlp)


def setup_inputs(seed: int = 0) -> dict:
    key = jax.random.key(seed)
    ks = jax.random.split(key, 24)
    f32 = jnp.float32

    def nrm(k, shape, scale=1.0):
        return jax.random.normal(k, shape, f32) * scale

    la = min(A_WINDOW_MAX, PAST_LEN)
    lc = min(C_WINDOW, PAST_LEN)
    dt = jnp.exp(jax.random.uniform(ks[10], (DEPTH, B_HEADS), f32, float(np.log(1e-3)), float(np.log(1e-1))))
    return {
        'x_prompt': nrm(ks[0], (BATCH, SEQ, D_MODEL)),
        'x_sample': nrm(ks[1], (DEC_BATCH, DEC_SEQ, D_MODEL)),
        'cache_dilated_kv': nrm(ks[2], (DEPTH, DEC_BATCH, la, 2, A_HEADS, HEAD_DIM)),
        'cache_swa_kv': nrm(ks[3], (DEPTH, DEC_BATCH, lc, 2, C_KV_HEADS, HEAD_DIM)),
        'state_delta_s': nrm(ks[4], (DEPTH, DEC_BATCH, B_HEADS, B_HEAD_DIM, B_HEAD_DIM), 0.1),
        'state_delta_conv': nrm(ks[5], (DEPTH, DEC_BATCH, CONV_W - 1, 3 * B_WIDTH)),
        'g_pre_mix': 1.0 + nrm(ks[6], (DEPTH, D_MODEL), 0.05),
        'w_in': nrm(ks[7], (DEPTH, D_MODEL, IN_COLS), D_MODEL ** -0.5),
        'delta_conv_w': nrm(ks[8], (DEPTH, CONV_W, 3 * B_WIDTH), CONV_W ** -0.5),
        'delta_a_log': jnp.log(jax.random.uniform(ks[9], (DEPTH, B_HEADS), f32, 1.0, 16.0)),
        'delta_dt_bias': dt + jnp.log(-jnp.expm1(-dt)),
        'delta_norm_g': 1.0 + nrm(ks[11], (DEPTH, B_HEAD_DIM), 0.05),
        'swa_sinks': nrm(ks[12], (DEPTH, C_HEADS)),
        'w_out': nrm(ks[13], (DEPTH, MIX_WIDTH, D_MODEL), MIX_WIDTH ** -0.5),
        'g_post_mix': 1.0 + nrm(ks[14], (DEPTH, D_MODEL), 0.05),
        'g_pre_mlp': 1.0 + nrm(ks[15], (DEPTH, D_MODEL), 0.05),
        'w_up': nrm(ks[16], (DEPTH, D_MODEL, D_FF), D_MODEL ** -0.5),
        'w_down': nrm(ks[17], (DEPTH, D_FF, D_MODEL), D_FF ** -0.5),
        'g_post_mlp': 1.0 + nrm(ks[18], (DEPTH, D_MODEL), 0.05),
    }


def reference(x_prompt, x_sample, cache_dilated_kv, cache_swa_kv, state_delta_s, state_delta_conv,
              g_pre_mix, w_in, delta_conv_w, delta_a_log, delta_dt_bias, delta_norm_g, swa_sinks,
              w_out, g_post_mix, g_pre_mlp, w_up, w_down, g_post_mlp):
    yp, ys = x_prompt, x_sample
    n_p = x_prompt.shape[0]
    p_akv, p_ckv, p_s, p_conv = [], [], [], []
    s_akv, s_ckv, s_s, s_conv = [], [], [], []
    for l in range(DEPTH):
        mixer_w = (w_in[l], delta_conv_w[l], delta_a_log[l], delta_dt_bias[l], delta_norm_g[l], swa_sinks[l])
        block_w = (w_out[l], g_post_mix[l], g_pre_mlp[l], w_up[l], w_down[l], g_post_mlp[l])
        conv0 = jnp.zeros((n_p, CONV_W - 1, 3 * B_WIDTH), yp.dtype)
        s0 = jnp.zeros((n_p, B_HEADS, B_HEAD_DIM, B_HEAD_DIM), state_delta_s.dtype)
        mix, akv, ckv, bs, bc = token_mixers(rms_norm(yp, g_pre_mix[l]), None, None, conv0, s0, *mixer_w)
        yp = sandwich_residual(yp, mix, *block_w)
        p_akv.append(akv); p_ckv.append(ckv); p_s.append(bs); p_conv.append(bc)
        mix, akv, ckv, bs, bc = token_mixers(rms_norm(ys, g_pre_mix[l]), cache_dilated_kv[l], cache_swa_kv[l],
                                             state_delta_conv[l], state_delta_s[l], *mixer_w)
        ys = sandwich_residual(ys, mix, *block_w)
        s_akv.append(akv); s_ckv.append(ckv); s_s.append(bs); s_conv.append(bc)
    return (yp, ys, jnp.stack(p_akv), jnp.stack(p_ckv), jnp.stack(p_s), jnp.stack(p_conv),
            jnp.stack(s_akv), jnp.stack(s_ckv), jnp.stack(s_s), jnp.stack(s_conv))
```

```python
import functools

import numpy as np
import jax
import jax.numpy as jnp
from jax import lax
from jax.experimental import pallas as pl
from jax.experimental.pallas import tpu as pltpu

F32 = jnp.float32
BF16 = jnp.bfloat16

D_MODEL = 2048
DEPTH = 4
HEAD_DIM = 64
A_HEADS = 12
A_BRANCHES = ((128, 1), (512, 4), (2048, 16))
A_WINDOW_MAX = 2048
B_HEAD_DIM = 128
B_HEADS = 4
CONV_W = 4
C_HEADS = 12
C_KV_HEADS = 4
C_WINDOW = 128
A_WIDTH = A_HEADS * HEAD_DIM
B_WIDTH = B_HEADS * B_HEAD_DIM
C_WIDTH = C_HEADS * HEAD_DIM
D_FF = 4 * D_MODEL
EPS = 1e-6
ATTN_SCALE = HEAD_DIM ** -0.5
NEG = -1e30

LANES = 128
SUBLANES = 8
VMEM_LIMIT = 56 * 1024 * 1024

Z_AQ, Z_AK, Z_AV = 0, 768, 1536
Z_CQ = 2304
Z_BQ, Z_BK, Z_BV, Z_BZ = 3072, 3584, 4096, 4608
Z_CK, Z_CV = 5120, 5376
Z_BA = 5632
Z_COLS = 5760
O_BQKV, O_BZ, O_BETA, O_CQ, O_CKV, O_END = 2304, 3840, 4352, 4360, 5128, 5640
C_HEAD_ORDER = (0, 3, 1, 4, 2, 5, 6, 9, 7, 10, 8, 11)
SAMPLE_ROWS = 8


def _alibi(n):
    return np.asarray([2.0 ** (-8.0 * (i + 1) / n) for i in range(n)], dtype=np.float32)


def _cparams(sem):
    return pltpu.CompilerParams(dimension_semantics=sem, vmem_limit_bytes=VMEM_LIMIT)


def _rms(x, g):
    return x * lax.rsqrt(jnp.mean(x * x, axis=-1, keepdims=True) + EPS) * g


def _sigmoid(x):
    return 1.0 / (1.0 + jnp.exp(-x))


def _inproj_kernel(x_ref, g_ref, w_ref, z_ref, h_scr):
    @pl.when(pl.program_id(1) == 0)
    def _():
        h_scr[...] = _rms(x_ref[...], g_ref[...]).astype(BF16)

    z_ref[...] = jnp.dot(h_scr[...], w_ref[...], preferred_element_type=F32)


def _inproj(x, g, w_all, layer, tm, tn):
    m = x.shape[0]
    return pl.pallas_call(
        _inproj_kernel,
        grid=(m // tm, Z_COLS // tn),
        in_specs=[
            pl.BlockSpec((tm, D_MODEL), lambda i, j: (i, 0)),
            pl.BlockSpec((1, D_MODEL), lambda i, j: (0, 0)),
            pl.BlockSpec((None, D_MODEL, tn), lambda i, j: (layer, 0, j)),
        ],
        out_specs=pl.BlockSpec((tm, tn), lambda i, j: (i, j)),
        out_shape=jax.ShapeDtypeStruct((m, Z_COLS), F32),
        scratch_shapes=[pltpu.VMEM((tm, D_MODEL), BF16)],
        compiler_params=_cparams(("parallel", "arbitrary")),
        name="inproj",
    )(x, g, w_all)


def _mlp_kernel(ma_ref, mb_ref, mc_ref, x_ref, wo_ref, gpm_ref, gpre_ref, wup_ref, wdn_ref, gpost_ref,
                y_ref, hm_scr, acc_scr):
    f = pl.program_id(1)

    @pl.when(f == 0)
    def _():
        y = jnp.dot(ma_ref[...].astype(BF16), wo_ref[0:A_WIDTH, :], preferred_element_type=F32)
        y += jnp.dot(mb_ref[...].astype(BF16), wo_ref[A_WIDTH:A_WIDTH + B_WIDTH, :],
                     preferred_element_type=F32)
        y += jnp.dot(mc_ref[...].astype(BF16), wo_ref[A_WIDTH + B_WIDTH:, :], preferred_element_type=F32)
        x1 = x_ref[...] + _rms(y, gpm_ref[...])
        y_ref[...] = x1
        hm_scr[...] = _rms(x1, gpre_ref[...]).astype(BF16)
        acc_scr[...] = jnp.zeros_like(acc_scr)

    u = jnp.dot(hm_scr[...], wup_ref[...], preferred_element_type=F32)
    u = jnp.square(jnp.maximum(u, 0.0)).astype(BF16)
    acc_scr[...] += jnp.dot(u, wdn_ref[...], preferred_element_type=F32)

    @pl.when(f == pl.num_programs(1) - 1)
    def _():
        y_ref[...] += _rms(acc_scr[...], gpost_ref[...])


def _mlp(ma, mb, mc, x, wo_all, gpm, gpre, wup_all, wdn_all, gpost, layer, tm, tf):
    m = x.shape[0]
    row = lambda i, f: (i, 0)
    vec = pl.BlockSpec((1, D_MODEL), lambda i, f: (0, 0))
    return pl.pallas_call(
        _mlp_kernel,
        grid=(m // tm, D_FF // tf),
        in_specs=[
            pl.BlockSpec((tm, A_WIDTH), row),
            pl.BlockSpec((tm, B_WIDTH), row),
            pl.BlockSpec((tm, C_WIDTH), row),
            pl.BlockSpec((tm, D_MODEL), row),
            pl.BlockSpec((None, D_MODEL, D_MODEL), lambda i, f: (layer, 0, 0)),
            vec, vec,
            pl.BlockSpec((None, D_MODEL, tf), lambda i, f: (layer, 0, f)),
            pl.BlockSpec((None, tf, D_MODEL), lambda i, f: (layer, f, 0)),
            vec,
        ],
        out_specs=pl.BlockSpec((tm, D_MODEL), row),
        out_shape=jax.ShapeDtypeStruct((m, D_MODEL), F32),
        scratch_shapes=[pltpu.VMEM((tm, D_MODEL), BF16), pltpu.VMEM((tm, D_MODEL), F32)],
        compiler_params=_cparams(("parallel", "arbitrary")),
        name="outproj_mlp",
    )(ma, mb, mc, x, wo_all, gpm, gpre, wup_all, wdn_all, gpost)


def _log_mult_a(d):
    c = np.zeros(d.shape, np.int64)
    for window, dil in A_BRANCHES:
        c += ((d >= 0) & (d <= window) & (d % dil == 0)).astype(np.int64)
    return np.where(c > 0, np.log(np.maximum(c, 1)), NEG).astype(np.float32)


def _log_mult_c(d):
    return np.where((d >= 0) & (d <= C_WINDOW), 0.0, NEG).astype(np.float32)


def _prompt_tables(log_mult, tq, tk, dmax):
    nwin = (dmax + tk - tq) // LANES + 1
    w = np.arange(nwin)[:, None, None]
    d = dmax - LANES * w + np.arange(tq)[None, :, None] - np.arange(tk)[None, None, :]
    return jnp.asarray(np.maximum(d, 0).astype(np.float32)), jnp.asarray(log_mult(d))


def _sample_tables(log_mult, past, rows):
    d_c = past + np.arange(rows)[:, None] - np.arange(past)[None, :]
    d_n = np.arange(rows)[:, None] - np.arange(LANES)[None, :]
    d_n = np.where(np.arange(LANES)[None, :] < rows, d_n, -1)
    f = lambda d: (jnp.asarray(np.maximum(d, 0).astype(np.float32)), jnp.asarray(log_mult(d)))
    return f(d_c) + f(d_n)


def _attn_prompt_kernel(slope_ref, sink_ref, q_ref, k_ref, v_ref, d_ref, l_ref, o_ref, *,
                        tq, tk, window, dmax, has_sink):
    p = pl.program_id(1)
    q0 = pl.program_id(2) * tq
    left = lax.broadcasted_iota(jnp.int32, (tq, LANES), 1) < HEAD_DIM
    q = q_ref[...] * ATTN_SCALE
    qh = (jnp.where(left, q, 0.0).astype(BF16), jnp.where(left, 0.0, q).astype(BF16))
    slopes = (slope_ref[p, 0], slope_ref[p, 1])
    hi = (q0 + tq - 1) // tk
    lo = jnp.maximum(q0 - window, 0) // tk

    def body(it, carry):
        k0 = pl.multiple_of((hi - it) * tk, tk)
        widx = (dmax - (q0 - k0)) // LANES
        k = k_ref[pl.ds(k0, tk), :].astype(BF16)
        v = v_ref[pl.ds(k0, tk), :].astype(BF16)
        dwin = d_ref[widx]
        lwin = l_ref[widx]
        new = []
        for h in range(2):
            m, l, acc = carry[h]
            s = lax.dot_general(qh[h], k, (((1,), (1,)), ((), ())), preferred_element_type=F32)
            s = s + (lwin - slopes[h] * dwin)
            m_new = jnp.maximum(m, jnp.max(s, axis=-1, keepdims=True))
            alpha = jnp.exp(m - m_new)
            pr = jnp.exp(s - m_new)
            l = alpha * l + jnp.sum(pr, axis=-1, keepdims=True)
            acc = alpha * acc + jnp.dot(pr.astype(BF16), v, preferred_element_type=F32)
            new.append((m_new, l, acc))
        return tuple(new)

    one = (jnp.full((tq, 1), -jnp.inf, F32), jnp.zeros((tq, 1), F32), jnp.zeros((tq, LANES), F32))
    res = lax.fori_loop(0, hi - lo + 1, body, (one, one))
    outs = []
    for h in range(2):
        m, l, acc = res[h]
        if has_sink:
            l = l + jnp.exp(sink_ref[p, h] - m)
        outs.append(acc / l)
    o_ref[...] = jnp.where(left, outs[0], outs[1]).astype(o_ref.dtype)


def _attn_prompt(z, slopes, sinks, tables, *, batch, seq, pairs, q_col, k_col, v_col, kv_of_pair,
                 tq, tk, window, dmax, has_sink, name):
    nq = seq // tq
    dtab, ltab = tables
    smem = pl.BlockSpec(memory_space=pltpu.SMEM)
    const3 = lambda b, p, i: (0, 0, 0)
    kern = functools.partial(_attn_prompt_kernel, tq=tq, tk=tk, window=window, dmax=dmax, has_sink=has_sink)
    return pl.pallas_call(
        kern,
        grid=(batch, pairs, nq),
        in_specs=[
            smem, smem,
            pl.BlockSpec((tq, LANES), lambda b, p, i: (b * nq + i, q_col // LANES + p)),
            pl.BlockSpec((seq, LANES), lambda b, p, i: (b, k_col // LANES + kv_of_pair(p))),
            pl.BlockSpec((seq, LANES), lambda b, p, i: (b, v_col // LANES + kv_of_pair(p))),
            pl.BlockSpec(dtab.shape, const3),
            pl.BlockSpec(ltab.shape, const3),
        ],
        out_specs=pl.BlockSpec((tq, LANES), lambda b, p, i: (b * nq + i, p)),
        out_shape=jax.ShapeDtypeStruct((batch * seq, pairs * LANES), BF16),
        compiler_params=_cparams(("parallel", "parallel", "arbitrary")),
        name=name,
    )(slopes, sinks, z, z, z, dtab, ltab)


def _attn_sample_kernel(slope_ref, sink_ref, q_ref, kn_ref, vn_ref, kc_ref, vc_ref,
                        dc_ref, lc_ref, dn_ref, ln_ref, o_ref, *, new_rows, has_sink):
    p = pl.program_id(1)
    rows = q_ref.shape[1]
    left = lax.broadcasted_iota(jnp.int32, (rows, LANES), 1) < HEAD_DIM
    q = q_ref[0] * ATTN_SCALE
    kc = kc_ref[0].astype(BF16)
    vc = vc_ref[0].astype(BF16)
    kn = kn_ref[0]
    vn = vn_ref[0]
    outs = []
    for h in range(2):
        qm = jnp.where(left, q, 0.0) if h == 0 else jnp.where(left, 0.0, q)
        slope = slope_ref[p, h]
        s_c = lax.dot_general(qm.astype(BF16), kc, (((1,), (1,)), ((), ())), preferred_element_type=F32)
        s_c = s_c + (lc_ref[...] - slope * dc_ref[...])
        bias_n = ln_ref[...] - slope * dn_ref[...]
        s_n = [jnp.sum(qm * kn[j:j + 1, :], axis=-1, keepdims=True) + bias_n[:, j:j + 1]
               for j in range(new_rows)]
        m = jnp.max(s_c, axis=-1, keepdims=True)
        for s in s_n:
            m = jnp.maximum(m, s)
        p_c = jnp.exp(s_c - m)
        l = jnp.sum(p_c, axis=-1, keepdims=True)
        o = jnp.dot(p_c.astype(BF16), vc, preferred_element_type=F32)
        for j, s in enumerate(s_n):
            p_n = jnp.exp(s - m)
            l = l + p_n
            o = o + p_n * vn[j:j + 1, :]
        if has_sink:
            l = l + jnp.exp(sink_ref[p, h] - m)
        outs.append(o / l)
    o_ref[0] = jnp.where(left, outs[0], outs[1])


def _attn_sample(z3, cache, slopes, sinks, tables, *, layer, pairs, q_col, k_col, v_col, ck_blk, cv_blk,
                 kv_of_pair, new_rows, has_sink, name):
    n, rows, _ = z3.shape
    past = cache.shape[2]
    smem = pl.BlockSpec(memory_space=pltpu.SMEM)
    const2 = lambda b, p: (0, 0)
    zblk = lambda col: pl.BlockSpec((1, rows, LANES), lambda b, p: (b, 0, col // LANES + kv_of_pair(p)))
    kern = functools.partial(_attn_sample_kernel, new_rows=new_rows, has_sink=has_sink)
    return pl.pallas_call(
        kern,
        grid=(n, pairs),
        in_specs=[
            smem, smem,
            pl.BlockSpec((1, rows, LANES), lambda b, p: (b, 0, q_col // LANES + p)),
            zblk(k_col), zblk(v_col),
            pl.BlockSpec((None, 1, past, LANES), lambda b, p: (layer, b, 0, ck_blk + kv_of_pair(p))),
            pl.BlockSpec((None, 1, past, LANES), lambda b, p: (layer, b, 0, cv_blk + kv_of_pair(p))),
        ] + [pl.BlockSpec(t.shape, const2) for t in tables],
        out_specs=pl.BlockSpec((1, rows, LANES), lambda b, p: (b, 0, p)),
        out_shape=jax.ShapeDtypeStruct((n, rows, pairs * LANES), F32),
        compiler_params=_cparams(("parallel", "arbitrary")),
        name=name,
    )(slopes, sinks, z3, z3, z3, cache, cache, *tables)


def _split(a):
    hi = a.astype(BF16)
    return hi, (a - hi.astype(F32)).astype(BF16)


_NN = (((1,), (0,)), ((), ()))
_NT = (((1,), (1,)), ((), ()))


def _mm(a, b, dims=_NN):
    dg = lambda x, y: lax.dot_general(x, y, dims, preferred_element_type=F32)
    return dg(a[0], b[0]) + dg(a[0], b[1]) + dg(a[1], b[0])


def _unit_lower_inverse(a, row, col):
    blk = lambda s: (row >> s) == (col >> s)
    b = jnp.where(blk(4), -a, 0.0)
    t = jnp.where(row == col, 1.0, 0.0) + b
    pw = b
    for _ in range(3):
        ps = _split(pw)
        pw = _mm(ps, ps)
        t = t + _mm(_split(t), _split(pw))
    for s in (4, 5, 6):
        off = jnp.where(blk(s + 1), jnp.where(blk(s), 0.0, a), 0.0)
        ts = _split(t)
        t = t - _mm(ts, _split(_mm(_split(off), ts)))
    return t


def _softplus(x):
    return jnp.maximum(x, 0.0) + jnp.log(1.0 + jnp.exp(-jnp.abs(x)))


def _lane_pick(x, idx):
    lane = lax.broadcasted_iota(jnp.int32, x.shape, 1)
    return jnp.sum(jnp.where(lane == idx, x, 0.0), axis=-1, keepdims=True)


def _delta_prompt_kernel(alog_ref, dtb_ref, bq_ref, bk_ref, bv_ref, bz_ref, ba_ref, cw_ref, cb_ref, s0_ref,
                         dng_ref, o_ref, s_ref, ext_scr, st_scr):
    c = pl.program_id(1)
    ch = bq_ref.shape[0]

    @pl.when(c == 0)
    def _():
        ext_scr[:, 0:SUBLANES, :] = jnp.zeros((3, SUBLANES, B_WIDTH), F32)
        for i in range(3):
            ext_scr[i, SUBLANES - (CONV_W - 1):SUBLANES, :] = cb_ref[0, :, i * B_WIDTH:(i + 1) * B_WIDTH]
        st_scr[...] = s0_ref[0]

    conv = []
    for i, ref in enumerate((bq_ref, bk_ref, bv_ref)):
        ext_scr[i, SUBLANES:SUBLANES + ch, :] = ref[...]
        acc = jnp.zeros((ch, B_WIDTH), F32)
        for j in range(CONV_W):
            w = cw_ref[j:j + 1, i * B_WIDTH:(i + 1) * B_WIDTH]
            acc = acc + ext_scr[i, pl.ds(SUBLANES - (CONV_W - 1) + j, ch), :] * w
        ext_scr[i, 0:SUBLANES, :] = ext_scr[i, ch:ch + SUBLANES, :]
        conv.append(acc * _sigmoid(acc))

    row = lax.broadcasted_iota(jnp.int32, (ch, ch), 0)
    col = lax.broadcasted_iota(jnp.int32, (ch, ch), 1)
    tri_incl = jnp.where(row >= col, 1.0, 0.0)
    tri_s = _split(tri_incl)
    ba = ba_ref[...]
    z = bz_ref[...]

    for h in range(B_HEADS):
        sl = slice(h * B_HEAD_DIM, (h + 1) * B_HEAD_DIM)
        q = conv[0][:, sl]
        k = conv[1][:, sl]
        v = conv[2][:, sl]
        q = q * lax.rsqrt(jnp.sum(q * q, axis=-1, keepdims=True) + EPS) * (B_HEAD_DIM ** -0.5)
        k = k * lax.rsqrt(jnp.sum(k * k, axis=-1, keepdims=True) + EPS)
        beta = _sigmoid(_lane_pick(ba, h))
        g = -jnp.exp(jnp.full((1, 1), alog_ref[h], F32)) * _softplus(_lane_pick(ba, B_HEADS + h) + dtb_ref[h])
        gb = jnp.broadcast_to(g, (ch, ch))
        g1 = gb.astype(BF16)
        r1 = gb - g1.astype(F32)
        g2 = r1.astype(BF16)
        g3 = (r1 - g2.astype(F32)).astype(BF16)
        dg = lambda x, y: lax.dot_general(x, y, _NN, preferred_element_type=F32)
        gc = dg(tri_s[0], g1) + dg(tri_s[0], g2) + dg(tri_s[0], g3)
        gct = gc.T
        decay = jnp.exp(jnp.where(row >= col, gc - gct, NEG))
        egc = jnp.exp(gc)
        kb = k * beta
        vb = v * beta
        ks = _split(k)
        a = jnp.where(row > col, _mm(_split(kb), ks, _NT) * decay, 0.0)
        t = _split(_unit_lower_inverse(a, row, col))
        u = _mm(t, _split(vb))
        w = _mm(t, _split(kb * egc))
        attn = _mm(_split(q), ks, _NT) * decay
        qd = q * egc
        gl = gc[ch - 1:ch, :]
        kd = k * jnp.exp(gl - gc)
        st = st_scr[h]
        ss = _split(st)
        v_new = u - _mm(_split(w), ss)
        vs = _split(v_new)
        o = _mm(_split(qd), ss) + _mm(_split(attn), vs)
        st_scr[h] = st * jnp.exp(gl) + _mm(_split(kd.T), vs)
        zz = z[:, sl]
        o_ref[:, sl] = (_rms(o, dng_ref[...]) * (zz * _sigmoid(zz))).astype(o_ref.dtype)

    @pl.when(c == pl.num_programs(1) - 1)
    def _():
        s_ref[0] = st_scr[...]


def _delta_prompt(z, conv_w, conv0, s0, a_log, dt_bias, dn_g, *, batch, seq, chunk):
    nc = seq // chunk
    smem = pl.BlockSpec(memory_space=pltpu.SMEM)
    zb = lambda col: pl.BlockSpec((chunk, B_WIDTH), lambda n, c: (n * nc + c, col // B_WIDTH))
    return pl.pallas_call(
        _delta_prompt_kernel,
        grid=(batch, nc),
        in_specs=[
            smem, smem,
            zb(Z_BQ), zb(Z_BK), zb(Z_BV), zb(Z_BZ),
            pl.BlockSpec((chunk, LANES), lambda n, c: (n * nc + c, Z_BA // LANES)),
            pl.BlockSpec(conv_w.shape, lambda n, c: (0, 0)),
            pl.BlockSpec((1, CONV_W - 1, 3 * B_WIDTH), lambda n, c: (n, 0, 0)),
            pl.BlockSpec((1, B_HEADS, B_HEAD_DIM, B_HEAD_DIM), lambda n, c: (n, 0, 0, 0)),
            pl.BlockSpec((1, B_HEAD_DIM), lambda n, c: (0, 0)),
        ],
        out_specs=[
            pl.BlockSpec((chunk, B_WIDTH), lambda n, c: (n * nc + c, 0)),
            pl.BlockSpec((1, B_HEADS, B_HEAD_DIM, B_HEAD_DIM), lambda n, c: (n, 0, 0, 0)),
        ],
        out_shape=[
            jax.ShapeDtypeStruct((batch * seq, B_WIDTH), BF16),
            jax.ShapeDtypeStruct((batch, B_HEADS, B_HEAD_DIM, B_HEAD_DIM), F32),
        ],
        scratch_shapes=[
            pltpu.VMEM((3, chunk + SUBLANES, B_WIDTH), F32),
            pltpu.VMEM((B_HEADS, B_HEAD_DIM, B_HEAD_DIM), F32),
        ],
        compiler_params=_cparams(("parallel", "arbitrary")),
        name="delta_prompt",
    )(a_log, dt_bias, z, z, z, z, z, conv_w, conv0, s0, dn_g)


def _delta_sample_kernel(alog_ref, dtb_ref, bq_ref, bk_ref, bv_ref, bz_ref, ba_ref,
                         wq_ref, wk_ref, wv_ref, cq_ref, ck_ref, cv_ref, s0_ref, dng_ref,
                         o_ref, s_ref, *, new_rows):
    h = pl.program_id(1)
    d = B_HEAD_DIM

    def conv_rows(x_ref, c_ref, w_ref):
        x = x_ref[0]
        cb = c_ref[0]
        ext = [cb[i:i + 1, :] for i in range(CONV_W - 1)] + [x[t:t + 1, :] for t in range(new_rows)]
        out = []
        for t in range(new_rows):
            acc = ext[t] * w_ref[0:1, :]
            for j in range(1, CONV_W):
                acc = acc + ext[t + j] * w_ref[j:j + 1, :]
            out.append(acc * _sigmoid(acc))
        return out

    qs = conv_rows(bq_ref, cq_ref, wq_ref)
    ks = conv_rows(bk_ref, ck_ref, wk_ref)
    vs = conv_rows(bv_ref, cv_ref, wv_ref)
    ba = ba_ref[0]
    z = bz_ref[0]
    beta_all = _sigmoid(_lane_pick(ba, h))
    g_all = -jnp.exp(jnp.full((1, 1), alog_ref[h], F32)) * _softplus(_lane_pick(ba, B_HEADS + h) + dtb_ref[h])
    eye = lax.broadcasted_iota(jnp.int32, (d, d), 0) == lax.broadcasted_iota(jnp.int32, (d, d), 1)
    to_col = lambda r: jnp.sum(jnp.where(eye, r, 0.0), axis=-1, keepdims=True)

    st = s0_ref[0, 0]
    o_ref[0] = jnp.zeros(o_ref.shape[1:], F32)
    for t in range(new_rows):
        q = qs[t] * lax.rsqrt(jnp.sum(qs[t] * qs[t], axis=-1, keepdims=True) + EPS) * (d ** -0.5)
        k = ks[t] * lax.rsqrt(jnp.sum(ks[t] * ks[t], axis=-1, keepdims=True) + EPS)
        a = jnp.exp(g_all[t:t + 1, :])
        b = beta_all[t:t + 1, :]
        kc = to_col(k)
        k_s = jnp.sum(kc * st, axis=0, keepdims=True)
        st = a * st + kc * (b * (vs[t] - a * k_s))
        o = jnp.sum(to_col(q) * st, axis=0, keepdims=True)
        zz = z[t:t + 1, :]
        o_ref[0, t:t + 1, :] = _rms(o, dng_ref[...]) * (zz * _sigmoid(zz))
    s_ref[0, 0] = st


def _delta_sample(z3, conv_w, conv_buf, s0, a_log, dt_bias, dn_g, *, layer, new_rows):
    n, rows, _ = z3.shape
    smem = pl.BlockSpec(memory_space=pltpu.SMEM)
    zb = lambda col: pl.BlockSpec((1, rows, LANES), lambda b, h: (b, 0, col // LANES + h))
    wb = lambda i: pl.BlockSpec((CONV_W, LANES), lambda b, h: (0, i * B_HEADS + h))
    cb = lambda i: pl.BlockSpec((None, 1, CONV_W - 1, LANES), lambda b, h: (layer, b, 0, i * B_HEADS + h))
    kern = functools.partial(_delta_sample_kernel, new_rows=new_rows)
    return pl.pallas_call(
        kern,
        grid=(n, B_HEADS),
        in_specs=[
            smem, smem,
            zb(Z_BQ), zb(Z_BK), zb(Z_BV), zb(Z_BZ),
            pl.BlockSpec((1, rows, LANES), lambda b, h: (b, 0, Z_BA // LANES)),
            wb(0), wb(1), wb(2), cb(0), cb(1), cb(2),
            pl.BlockSpec((None, 1, 1, B_HEAD_DIM, B_HEAD_DIM), lambda b, h: (layer, b, h, 0, 0)),
            pl.BlockSpec((1, B_HEAD_DIM), lambda b, h: (0, 0)),
        ],
        out_specs=[
            pl.BlockSpec((1, rows, LANES), lambda b, h: (b, 0, h)),
            pl.BlockSpec((1, 1, B_HEAD_DIM, B_HEAD_DIM), lambda b, h: (b, h, 0, 0)),
        ],
        out_shape=[
            jax.ShapeDtypeStruct((n, rows, B_WIDTH), F32),
            jax.ShapeDtypeStruct((n, B_HEADS, B_HEAD_DIM, B_HEAD_DIM), F32),
        ],
        compiler_params=_cparams(("parallel", "arbitrary")),
        name="delta_sample",
    )(a_log, dt_bias, z3, z3, z3, z3, z3, conv_w, conv_w, conv_w, conv_buf, conv_buf, conv_buf, s0, dn_g)


def _layout_w_in(w_in):
    order = np.asarray(C_HEAD_ORDER)
    cq = w_in[:, :, O_CQ:O_CKV].reshape(DEPTH, D_MODEL, C_HEADS, HEAD_DIM)[:, :, order]
    cq = cq.reshape(DEPTH, D_MODEL, C_WIDTH)
    pad = jnp.zeros((DEPTH, D_MODEL, Z_COLS - Z_BA - 2 * B_HEADS), w_in.dtype)
    parts = [w_in[:, :, :O_BQKV], cq, w_in[:, :, O_BQKV:O_BETA], w_in[:, :, O_CKV:O_END],
             w_in[:, :, O_BETA:O_CQ], pad]
    return jnp.concatenate(parts, axis=-1).astype(BF16)


def _layout_w_out(w_out):
    order = np.asarray(C_HEAD_ORDER)
    c_rows = w_out[:, A_WIDTH + B_WIDTH:].reshape(DEPTH, C_HEADS, HEAD_DIM, D_MODEL)[:, order]
    c_rows = c_rows.reshape(DEPTH, C_WIDTH, D_MODEL)
    return jnp.concatenate([w_out[:, :A_WIDTH + B_WIDTH], c_rows], axis=1).astype(BF16)


def _pair_table(v):
    return jnp.asarray(v, F32).reshape(-1, 2)


def kernel(x_prompt, x_sample, cache_dilated_kv, cache_swa_kv, state_delta_s, state_delta_conv, g_pre_mix,
           w_in, delta_conv_w, delta_a_log, delta_dt_bias, delta_norm_g, swa_sinks, w_out, g_post_mix,
           g_pre_mlp, w_up, w_down, g_post_mlp):
    batch, seq, _ = x_prompt.shape
    dec_batch, dec_seq, _ = x_sample.shape
    past_a = cache_dilated_kv.shape[2]
    past_c = cache_swa_kv.shape[2]
    assert seq % 256 == 0 and seq >= A_WINDOW_MAX and dec_seq <= SAMPLE_ROWS
    assert past_a == A_WINDOW_MAX and past_c == C_WINDOW

    w_in_p = _layout_w_in(w_in)
    w_out_p = _layout_w_out(w_out)
    w_up_b = w_up.astype(BF16)
    w_dn_b = w_down.astype(BF16)
    cache_a = cache_dilated_kv.reshape(DEPTH, dec_batch, past_a, 2 * A_WIDTH)
    cache_c = cache_swa_kv.reshape(DEPTH, dec_batch, past_c, 2 * C_KV_HEADS * HEAD_DIM)

    tq = 128
    a_tk, a_dmax = 256, seq - tq
    c_tk, c_dmax = 128, C_WINDOW
    tab_pa = _prompt_tables(_log_mult_a, tq, a_tk, a_dmax)
    tab_pc = _prompt_tables(_log_mult_c, tq, c_tk, c_dmax)
    tab_sa = _sample_tables(_log_mult_a, past_a, SAMPLE_ROWS)
    tab_sc = _sample_tables(_log_mult_c, past_c, SAMPLE_ROWS)
    slopes_a = _pair_table(_alibi(A_HEADS))
    slopes_c = _pair_table(_alibi(C_HEADS)[np.asarray(C_HEAD_ORDER)])
    no_sinks = jnp.zeros_like(slopes_a)
    c_order = np.asarray(C_HEAD_ORDER)

    conv0 = jnp.zeros((batch, CONV_W - 1, 3 * B_WIDTH), F32)
    s_zero = jnp.zeros((batch, B_HEADS, B_HEAD_DIM, B_HEAD_DIM), F32)

    xp = x_prompt.reshape(batch * seq, D_MODEL)
    xs = jnp.pad(x_sample, ((0, 0), (0, SAMPLE_ROWS - dec_seq), (0, 0))).reshape(dec_batch * SAMPLE_ROWS, D_MODEL)
    ms = xs.shape[0]

    p_akv, p_ckv, p_s, p_conv = [], [], [], []
    s_akv, s_ckv, s_s, s_conv = [], [], [], []
    for l in range(DEPTH):
        row = lambda a: a[l].reshape(1, -1)
        sinks_c = _pair_table(swa_sinks[l][c_order])
        zp = _inproj(xp, row(g_pre_mix), w_in_p, l, 512, 1920)
        zs = _inproj(xs, row(g_pre_mix), w_in_p, l, ms, 1920)
        zs3 = zs.reshape(dec_batch, SAMPLE_ROWS, Z_COLS)

        a_p = _attn_prompt(zp, slopes_a, no_sinks, tab_pa, batch=batch, seq=seq, pairs=A_HEADS // 2,
                           q_col=Z_AQ, k_col=Z_AK, v_col=Z_AV, kv_of_pair=lambda p: p, tq=tq, tk=a_tk,
                           window=A_WINDOW_MAX, dmax=a_dmax, has_sink=False, name="attn_a_prompt")
        c_p = _attn_prompt(zp, slopes_c, sinks_c, tab_pc, batch=batch, seq=seq, pairs=C_HEADS // 2,
                           q_col=Z_CQ, k_col=Z_CK, v_col=Z_CV, kv_of_pair=lambda p: p // 3, tq=tq, tk=c_tk,
                           window=C_WINDOW, dmax=c_dmax, has_sink=True, name="attn_c_prompt")
        b_p, st_p = _delta_prompt(zp, delta_conv_w[l], conv0, s_zero, delta_a_log[l], delta_dt_bias[l],
                                  row(delta_norm_g), batch=batch, seq=seq, chunk=128)

        a_s = _attn_sample(zs3, cache_a, slopes_a, no_sinks, tab_sa, layer=l, pairs=A_HEADS // 2,
                           q_col=Z_AQ, k_col=Z_AK, v_col=Z_AV, ck_blk=0, cv_blk=A_WIDTH // LANES,
                           kv_of_pair=lambda p: p, new_rows=dec_seq, has_sink=False, name="attn_a_sample")
        c_s = _attn_sample(zs3, cache_c, slopes_c, sinks_c, tab_sc, layer=l, pairs=C_HEADS // 2,
                           q_col=Z_CQ, k_col=Z_CK, v_col=Z_CV, ck_blk=0,
                           cv_blk=C_KV_HEADS * HEAD_DIM // LANES, kv_of_pair=lambda p: p // 3,
                           new_rows=dec_seq, has_sink=True, name="attn_c_sample")
        b_s, st_s = _delta_sample(zs3, delta_conv_w[l], state_delta_conv, state_delta_s, delta_a_log[l],
                                  delta_dt_bias[l], row(delta_norm_g), layer=l, new_rows=dec_seq)

        mlp_w = (w_out_p, row(g_post_mix), row(g_pre_mlp), w_up_b, w_dn_b, row(g_post_mlp))
        xp = _mlp(a_p, b_p, c_p, xp, *mlp_w, l, 512, 512)
        xs = _mlp(a_s.reshape(ms, A_WIDTH), b_s.reshape(ms, B_WIDTH), c_s.reshape(ms, C_WIDTH), xs,
                  *mlp_w, l, ms, 512)

        zp3 = zp.reshape(batch, seq, Z_COLS)
        p_akv.append(zp3[:, seq - A_WINDOW_MAX:, Z_AK:Z_AK + 2 * A_WIDTH]
                     .reshape(batch, A_WINDOW_MAX, 2, A_HEADS, HEAD_DIM))
        p_ckv.append(zp3[:, seq - C_WINDOW:, Z_CK:Z_CK + 2 * C_KV_HEADS * HEAD_DIM]
                     .reshape(batch, C_WINDOW, 2, C_KV_HEADS, HEAD_DIM))
        p_s.append(st_p)
        p_conv.append(zp3[:, seq - (CONV_W - 1):, Z_BQ:Z_BQ + 3 * B_WIDTH])
        s_akv.append(zs3[:, :dec_seq, Z_AK:Z_AK + 2 * A_WIDTH].reshape(dec_batch, dec_seq, 2, A_HEADS, HEAD_DIM))
        s_ckv.append(zs3[:, :dec_seq, Z_CK:Z_CK + 2 * C_KV_HEADS * HEAD_DIM]
                     .reshape(dec_batch, dec_seq, 2, C_KV_HEADS, HEAD_DIM))
        s_s.append(st_s)
        full = jnp.concatenate([state_delta_conv[l], zs3[:, :dec_seq, Z_BQ:Z_BQ + 3 * B_WIDTH]], axis=1)
        s_conv.append(full[:, full.shape[1] - (CONV_W - 1):])

    yp = xp.reshape(batch, seq, D_MODEL)
    ys = xs.reshape(dec_batch, SAMPLE_ROWS, D_MODEL)[:, :dec_seq]
    return (yp, ys, jnp.stack(p_akv), jnp.stack(p_ckv), jnp.stack(p_s), jnp.stack(p_conv),
            jnp.stack(s_akv), jnp.stack(s_ckv), jnp.stack(s_s), jnp.stack(s_conv))
```

```python
import functools

import numpy as np
import jax
import jax.numpy as jnp
from jax import lax
from jax.experimental import pallas as pl
from jax.experimental.pallas import tpu as pltpu

F32 = jnp.float32
BF16 = jnp.bfloat16

D_MODEL = 2048
DEPTH = 4
HEAD_DIM = 64
A_HEADS = 12
A_BRANCHES = ((128, 1), (512, 4), (2048, 16))
A_WINDOW_MAX = 2048
B_HEAD_DIM = 128
B_HEADS = 4
CONV_W = 4
C_HEADS = 12
C_KV_HEADS = 4
C_WINDOW = 128
A_WIDTH = A_HEADS * HEAD_DIM
B_WIDTH = B_HEADS * B_HEAD_DIM
C_WIDTH = C_HEADS * HEAD_DIM
C_KV_WIDTH = C_KV_HEADS * HEAD_DIM
D_FF = 4 * D_MODEL
EPS = 1e-6
ATTN_SCALE = HEAD_DIM ** -0.5
NEG = -1e30

LANES = 128
SUBLANES = 8
VMEM_LIMIT = 56 * 1024 * 1024

Z_AQ, Z_AK, Z_AV = 0, 768, 1536
Z_CQ = 2304
Z_BQ, Z_BK, Z_BV, Z_BZ = 3072, 3584, 4096, 4608
Z_CK, Z_CV = 5120, 5376
Z_BA = 5632
Z_COLS = 5760
O_BQKV, O_BZ, O_BETA, O_CQ, O_CKV, O_END = 2304, 3840, 4352, 4360, 5128, 5640
C_HEAD_ORDER = (0, 3, 1, 4, 2, 5, 6, 9, 7, 10, 8, 11)
SAMPLE_ROWS = 8
TQ = 128
A_CLASS_BLOCKS = 4

_NN = (((1,), (0,)), ((), ()))
_NT = (((1,), (1,)), ((), ()))


def _alibi(n):
    return np.asarray([2.0 ** (-8.0 * (i + 1) / n) for i in range(n)], dtype=np.float32)


def _cparams(sem):
    return pltpu.CompilerParams(dimension_semantics=sem, vmem_limit_bytes=VMEM_LIMIT)


def _rms(x, g):
    return x * lax.rsqrt(jnp.mean(x * x, axis=-1, keepdims=True) + EPS) * g


def _sigmoid(x):
    return 1.0 / (1.0 + jnp.exp(-x))


def _inproj_kernel(x_ref, g_ref, w_ref, z_ref, h_scr):
    @pl.when(pl.program_id(1) == 0)
    def _():
        h_scr[...] = _rms(x_ref[...], g_ref[...]).astype(BF16)

    z_ref[...] = jnp.dot(h_scr[...], w_ref[...], preferred_element_type=F32)


def _inproj(x, g, w_all, layer, tm, tn):
    m = x.shape[0]
    return pl.pallas_call(
        _inproj_kernel,
        grid=(m // tm, Z_COLS // tn),
        in_specs=[
            pl.BlockSpec((tm, D_MODEL), lambda i, j: (i, 0)),
            pl.BlockSpec((1, D_MODEL), lambda i, j: (0, 0)),
            pl.BlockSpec((None, D_MODEL, tn), lambda i, j: (layer, 0, j)),
        ],
        out_specs=pl.BlockSpec((tm, tn), lambda i, j: (i, j)),
        out_shape=jax.ShapeDtypeStruct((m, Z_COLS), F32),
        scratch_shapes=[pltpu.VMEM((tm, D_MODEL), BF16)],
        compiler_params=_cparams(("parallel", "arbitrary")),
        name="inproj",
    )(x, g, w_all)


def _mlp_kernel(ma_ref, mb_ref, mc_ref, x_ref, wo_ref, gpm_ref, gpre_ref, wup_ref, wdn_ref, gpost_ref,
                y_ref, hm_scr, acc_scr):
    f = pl.program_id(1)

    @pl.when(f == 0)
    def _():
        y = jnp.dot(ma_ref[...].astype(BF16), wo_ref[0:A_WIDTH, :], preferred_element_type=F32)
        y += jnp.dot(mb_ref[...].astype(BF16), wo_ref[A_WIDTH:A_WIDTH + B_WIDTH, :],
                     preferred_element_type=F32)
        y += jnp.dot(mc_ref[...].astype(BF16), wo_ref[A_WIDTH + B_WIDTH:, :], preferred_element_type=F32)
        x1 = x_ref[...] + _rms(y, gpm_ref[...])
        y_ref[...] = x1
        hm_scr[...] = _rms(x1, gpre_ref[...]).astype(BF16)
        acc_scr[...] = jnp.zeros_like(acc_scr)

    u = jnp.dot(hm_scr[...], wup_ref[...], preferred_element_type=F32)
    u = jnp.square(jnp.maximum(u, 0.0)).astype(BF16)
    acc_scr[...] += jnp.dot(u, wdn_ref[...], preferred_element_type=F32)

    @pl.when(f == pl.num_programs(1) - 1)
    def _():
        y_ref[...] += _rms(acc_scr[...], gpost_ref[...])


def _mlp(ma, mb, mc, x, wo_all, gpm, gpre, wup_all, wdn_all, gpost, layer, tm, tf):
    m = x.shape[0]
    row = lambda i, f: (i, 0)
    vec = pl.BlockSpec((1, D_MODEL), lambda i, f: (0, 0))
    return pl.pallas_call(
        _mlp_kernel,
        grid=(m // tm, D_FF // tf),
        in_specs=[
            pl.BlockSpec((tm, A_WIDTH), row),
            pl.BlockSpec((tm, B_WIDTH), row),
            pl.BlockSpec((tm, C_WIDTH), row),
            pl.BlockSpec((tm, D_MODEL), row),
            pl.BlockSpec((None, D_MODEL, D_MODEL), lambda i, f: (layer, 0, 0)),
            vec, vec,
            pl.BlockSpec((None, D_MODEL, tf), lambda i, f: (layer, 0, f)),
            pl.BlockSpec((None, tf, D_MODEL), lambda i, f: (layer, f, 0)),
            vec,
        ],
        out_specs=pl.BlockSpec((tm, D_MODEL), row),
        out_shape=jax.ShapeDtypeStruct((m, D_MODEL), F32),
        scratch_shapes=[pltpu.VMEM((tm, D_MODEL), BF16), pltpu.VMEM((tm, D_MODEL), F32)],
        compiler_params=_cparams(("parallel", "arbitrary")),
        name="outproj_mlp",
    )(ma, mb, mc, x, wo_all, gpm, gpre, wup_all, wdn_all, gpost)


def _log_mult_a(d):
    c = np.zeros(d.shape, np.int64)
    for window, dil in A_BRANCHES:
        c += ((d >= 0) & (d <= window) & (d % dil == 0)).astype(np.int64)
    return np.where(c > 0, np.log(np.maximum(c, 1)), NEG).astype(np.float32)


def _log_mult_c(d):
    return np.where((d >= 0) & (d <= C_WINDOW), 0.0, NEG).astype(np.float32)


def _tables(log_mult, d):
    return jnp.asarray(np.maximum(d, 0).astype(np.float32)), jnp.asarray(log_mult(d))


def _prompt_tables_a(seq):
    d = np.arange(TQ)[:, None] - np.arange(seq)[None, :] + (seq - TQ)
    return _tables(_log_mult_a, d)


def _prompt_tables_c():
    i = np.arange(TQ)[:, None]
    j = np.arange(2 * TQ)[None, :]
    return _tables(_log_mult_c, np.stack([TQ + i - j, i - j]))


def _sample_tables(log_mult, past, rows):
    d_c = past + np.arange(rows)[:, None] - np.arange(past)[None, :]
    d_n = np.arange(rows)[:, None] - np.arange(LANES)[None, :]
    d_n = np.where(np.arange(LANES)[None, :] < rows, d_n, -1)
    return _tables(log_mult, d_c) + _tables(log_mult, d_n)


def _head_masks(rows):
    left = lax.broadcasted_iota(jnp.int32, (rows, LANES), 1) < HEAD_DIM
    return left, (lambda x: jnp.where(left, x, 0.0), lambda x: jnp.where(left, 0.0, x))


def _softmax_pv(s, v, extra=None):
    m = jnp.max(s, axis=-1, keepdims=True)
    pr = jnp.exp(s - m)
    l = jnp.sum(pr, axis=-1, keepdims=True)
    if extra is not None:
        l = l + jnp.exp(extra - m)
    return jnp.dot(pr.astype(BF16), v, preferred_element_type=F32) / l


def _attn_a_prompt_kernel(slope_ref, q_ref, k_ref, v_ref, d_ref, l_ref, o_ref, bias_scr, k_scr, v_scr, *,
                          n_cls):
    p = pl.program_id(1)
    c = pl.program_id(2)
    seq = k_ref.shape[0]
    left, pick = _head_masks(TQ)

    @pl.when(c == 0)
    def _():
        for h in range(2):
            bias_scr[h] = l_ref[...] - slope_ref[p, h] * d_ref[...]
        k_scr[...] = k_ref[...].astype(BF16)
        v_scr[...] = v_ref[...].astype(BF16)

    for cc in range(n_cls):
        @pl.when(c == cc)
        def _(cc=cc):
            blocks = []
            for j in range(A_CLASS_BLOCKS):
                g = cc * A_CLASS_BLOCKS + j
                keys = TQ * (g + 1)
                off = seq - keys
                q = q_ref[j * TQ:(j + 1) * TQ, :] * ATTN_SCALE
                k = k_scr[0:keys, :]
                s = [lax.dot_general(pick[h](q).astype(BF16), k, _NT, preferred_element_type=F32)
                     + bias_scr[h, :, off:off + keys] for h in range(2)]
                blocks.append((j, keys, s))
            for j, keys, s in blocks:
                v = v_scr[0:keys, :]
                o = [_softmax_pv(s[h], v) for h in range(2)]
                o_ref[j * TQ:(j + 1) * TQ, :] = jnp.where(left, o[0], o[1]).astype(o_ref.dtype)


def _attn_a_prompt(z, slopes, tables, *, batch, seq):
    pairs = A_HEADS // 2
    rows = TQ * A_CLASS_BLOCKS
    n_cls = seq // rows
    dtab, ltab = tables
    const2 = lambda b, p, c: (0, 0)
    kern = functools.partial(_attn_a_prompt_kernel, n_cls=n_cls)
    return pl.pallas_call(
        kern,
        grid=(batch, pairs, n_cls),
        in_specs=[
            pl.BlockSpec(memory_space=pltpu.SMEM),
            pl.BlockSpec((rows, LANES), lambda b, p, c: (b * n_cls + c, Z_AQ // LANES + p)),
            pl.BlockSpec((seq, LANES), lambda b, p, c: (b, Z_AK // LANES + p)),
            pl.BlockSpec((seq, LANES), lambda b, p, c: (b, Z_AV // LANES + p)),
            pl.BlockSpec(dtab.shape, const2),
            pl.BlockSpec(ltab.shape, const2),
        ],
        out_specs=pl.BlockSpec((rows, LANES), lambda b, p, c: (b * n_cls + c, p)),
        out_shape=jax.ShapeDtypeStruct((batch * seq, A_WIDTH), BF16),
        scratch_shapes=[pltpu.VMEM((2, TQ, seq), F32), pltpu.VMEM((seq, LANES), BF16),
                        pltpu.VMEM((seq, LANES), BF16)],
        compiler_params=_cparams(("parallel", "parallel", "arbitrary")),
        name="attn_a_prompt",
    )(slopes, z, z, z, dtab, ltab)


def _attn_c_prompt_kernel(slope_ref, sink_ref, q_ref, k_ref, v_ref, d_ref, l_ref, o_ref, *, blocks):
    p = pl.program_id(1)
    c = pl.program_id(2)
    base = c * (blocks * TQ)
    left, pick = _head_masks(TQ)
    first = c == 0
    bias = [[l_ref[t] - slope_ref[p, h] * d_ref[t] for t in range(2)] for h in range(2)]
    work = []
    for j in range(blocks):
        start = jnp.maximum(base - TQ, 0) if j == 0 else base + (j - 1) * TQ
        start = pl.multiple_of(start, TQ)
        q = q_ref[j * TQ:(j + 1) * TQ, :] * ATTN_SCALE
        k = k_ref[pl.ds(start, 2 * TQ), :].astype(BF16)
        s = []
        for h in range(2):
            b = jnp.where(first, bias[h][1], bias[h][0]) if j == 0 else bias[h][0]
            s.append(lax.dot_general(pick[h](q).astype(BF16), k, _NT, preferred_element_type=F32) + b)
        work.append((j, start, s))
    for j, start, s in work:
        v = v_ref[pl.ds(start, 2 * TQ), :].astype(BF16)
        o = [_softmax_pv(s[h], v, extra=sink_ref[p, h]) for h in range(2)]
        o_ref[j * TQ:(j + 1) * TQ, :] = jnp.where(left, o[0], o[1]).astype(o_ref.dtype)


def _attn_c_prompt(z, slopes, sinks, tables, *, batch, seq, blocks):
    pairs = C_HEADS // 2
    rows = TQ * blocks
    nchunk = seq // rows
    dtab, ltab = tables
    smem = pl.BlockSpec(memory_space=pltpu.SMEM)
    const3 = lambda b, p, c: (0, 0, 0)
    kern = functools.partial(_attn_c_prompt_kernel, blocks=blocks)
    return pl.pallas_call(
        kern,
        grid=(batch, pairs, nchunk),
        in_specs=[
            smem, smem,
            pl.BlockSpec((rows, LANES), lambda b, p, c: (b * nchunk + c, Z_CQ // LANES + p)),
            pl.BlockSpec((seq, LANES), lambda b, p, c: (b, Z_CK // LANES + p // 3)),
            pl.BlockSpec((seq, LANES), lambda b, p, c: (b, Z_CV // LANES + p // 3)),
            pl.BlockSpec(dtab.shape, const3),
            pl.BlockSpec(ltab.shape, const3),
        ],
        out_specs=pl.BlockSpec((rows, LANES), lambda b, p, c: (b * nchunk + c, p)),
        out_shape=jax.ShapeDtypeStruct((batch * seq, C_WIDTH), BF16),
        compiler_params=_cparams(("parallel", "parallel", "arbitrary")),
        name="attn_c_prompt",
    )(slopes, sinks, z, z, z, dtab, ltab)


def _attn_sample_kernel(slope_ref, sink_ref, q_ref, kn_ref, vn_ref, kc_ref, vc_ref,
                        dc_ref, lc_ref, dn_ref, ln_ref, o_ref, *, new_rows, has_sink):
    p = pl.program_id(1)
    rows = q_ref.shape[1]
    left, pick = _head_masks(rows)
    q = q_ref[0] * ATTN_SCALE
    kc = kc_ref[0].astype(BF16)
    vc = vc_ref[0].astype(BF16)
    kn = kn_ref[0]
    vn = vn_ref[0]
    qm = [pick[h](q) for h in range(2)]
    s_c = [jnp.dot(qm[h].astype(BF16), kc, preferred_element_type=F32)
           + (lc_ref[...] - slope_ref[p, h] * dc_ref[...]) for h in range(2)]
    outs = []
    for h in range(2):
        bias_n = ln_ref[...] - slope_ref[p, h] * dn_ref[...]
        s_n = [jnp.sum(qm[h] * kn[j:j + 1, :], axis=-1, keepdims=True) + bias_n[:, j:j + 1]
               for j in range(new_rows)]
        m = jnp.max(s_c[h], axis=-1, keepdims=True)
        for s in s_n:
            m = jnp.maximum(m, s)
        p_c = jnp.exp(s_c[h] - m)
        l = jnp.sum(p_c, axis=-1, keepdims=True)
        o = lax.dot_general(p_c.astype(BF16), vc, _NT, preferred_element_type=F32)
        for j, s in enumerate(s_n):
            p_n = jnp.exp(s - m)
            l = l + p_n
            o = o + p_n * vn[j:j + 1, :]
        if has_sink:
            l = l + jnp.exp(sink_ref[p, h] - m)
        outs.append(o / l)
    o_ref[0] = jnp.where(left, outs[0], outs[1])


def _attn_sample(z3, cache_t, slopes, sinks, tables, *, layer, pairs, q_col, k_col, v_col, v_blk,
                 kv_of_pair, new_rows, has_sink, name):
    n, rows, _ = z3.shape
    past = cache_t.shape[3]
    smem = pl.BlockSpec(memory_space=pltpu.SMEM)
    const2 = lambda b, p: (0, 0)
    zblk = lambda col: pl.BlockSpec((1, rows, LANES), lambda b, p: (b, 0, col // LANES + kv_of_pair(p)))
    kern = functools.partial(_attn_sample_kernel, new_rows=new_rows, has_sink=has_sink)
    return pl.pallas_call(
        kern,
        grid=(n, pairs),
        in_specs=[
            smem, smem,
            pl.BlockSpec((1, rows, LANES), lambda b, p: (b, 0, q_col // LANES + p)),
            zblk(k_col), zblk(v_col),
            pl.BlockSpec((None, 1, LANES, past), lambda b, p: (layer, b, kv_of_pair(p), 0)),
            pl.BlockSpec((None, 1, LANES, past), lambda b, p: (layer, b, v_blk + kv_of_pair(p), 0)),
        ] + [pl.BlockSpec(t.shape, const2) for t in tables],
        out_specs=pl.BlockSpec((1, rows, LANES), lambda b, p: (b, 0, p)),
        out_shape=jax.ShapeDtypeStruct((n, rows, pairs * LANES), F32),
        compiler_params=_cparams(("parallel", "arbitrary")),
        name=name,
    )(slopes, sinks, z3, z3, z3, cache_t, cache_t, *tables)


def _split(a):
    hi = a.astype(BF16)
    return hi, (a - hi.astype(F32)).astype(BF16)


def _mm(a, b, dims=_NN):
    dg = lambda x, y: lax.dot_general(x, y, dims, preferred_element_type=F32)
    return dg(a[0], b[0]) + dg(a[0], b[1]) + dg(a[1], b[0])


def _mm1(a, b, dims=_NN):
    return lax.dot_general(a.astype(BF16), b.astype(BF16), dims, preferred_element_type=F32)


def _unit_lower_inverse(mats, row, col):
    blk = lambda s: (row >> s) == (col >> s)
    eye = jnp.where(row == col, 1.0, 0.0)
    b = [jnp.where(blk(4), -a, 0.0) for a in mats]
    t = [eye + x for x in b]
    ps = [_split(x) for x in b]
    for _ in range(3):
        ps = [_split(_mm(x, x)) for x in ps]
        t = [x + _mm(_split(x), y) for x, y in zip(t, ps)]
    for s in (4, 5, 6):
        off = [jnp.where(blk(s + 1), jnp.where(blk(s), 0.0, a), 0.0) for a in mats]
        ts = [_split(x) for x in t]
        mid = [_split(_mm(_split(o), y)) for o, y in zip(off, ts)]
        t = [x - _mm(y, m) for x, y, m in zip(t, ts, mid)]
    return t


def _softplus(x):
    return jnp.maximum(x, 0.0) + jnp.log(1.0 + jnp.exp(-jnp.abs(x)))


def _lane_pick(x, idx):
    lane = lax.broadcasted_iota(jnp.int32, x.shape, 1)
    return jnp.sum(jnp.where(lane == idx, x, 0.0), axis=-1, keepdims=True)


def _delta_prompt_kernel(alog_ref, dtb_ref, bq_ref, bk_ref, bv_ref, bz_ref, ba_ref, cw_ref, cb_ref, s0_ref,
                         dng_ref, o_ref, s_ref, ext_scr, st_scr):
    c = pl.program_id(1)
    ch = bq_ref.shape[0]
    heads = range(B_HEADS)

    @pl.when(c == 0)
    def _():
        ext_scr[:, 0:SUBLANES, :] = jnp.zeros((3, SUBLANES, B_WIDTH), F32)
        for i in range(3):
            ext_scr[i, SUBLANES - (CONV_W - 1):SUBLANES, :] = cb_ref[0, :, i * B_WIDTH:(i + 1) * B_WIDTH]
        st_scr[...] = s0_ref[0]

    conv = []
    for i, ref in enumerate((bq_ref, bk_ref, bv_ref)):
        ext_scr[i, SUBLANES:SUBLANES + ch, :] = ref[...]
        acc = jnp.zeros((ch, B_WIDTH), F32)
        for j in range(CONV_W):
            w = cw_ref[j:j + 1, i * B_WIDTH:(i + 1) * B_WIDTH]
            acc = acc + ext_scr[i, pl.ds(SUBLANES - (CONV_W - 1) + j, ch), :] * w
        ext_scr[i, 0:SUBLANES, :] = ext_scr[i, ch:ch + SUBLANES, :]
        conv.append(acc * _sigmoid(acc))

    row = lax.broadcasted_iota(jnp.int32, (ch, ch), 0)
    col = lax.broadcasted_iota(jnp.int32, (ch, ch), 1)
    tri = jnp.where(row >= col, 1.0, 0.0).astype(BF16)
    ba = ba_ref[...]
    dg = lambda x, y: lax.dot_general(x, y, _NN, preferred_element_type=F32)

    sl = [slice(h * B_HEAD_DIM, (h + 1) * B_HEAD_DIM) for h in heads]
    q = [conv[0][:, s] for s in sl]
    k = [conv[1][:, s] for s in sl]
    v = [conv[2][:, s] for s in sl]
    q = [x * lax.rsqrt(jnp.sum(x * x, axis=-1, keepdims=True) + EPS) * (B_HEAD_DIM ** -0.5) for x in q]
    k = [x * lax.rsqrt(jnp.sum(x * x, axis=-1, keepdims=True) + EPS) for x in k]
    beta = [_sigmoid(_lane_pick(ba, h)) for h in heads]

    gparts = []
    for h in heads:
        g = -jnp.exp(jnp.full((1, 1), alog_ref[h], F32)) * _softplus(_lane_pick(ba, B_HEADS + h) + dtb_ref[h])
        gb = jnp.broadcast_to(g, (ch, ch))
        g1 = gb.astype(BF16)
        r1 = gb - g1.astype(F32)
        g2 = r1.astype(BF16)
        gparts.append((g1, g2, (r1 - g2.astype(F32)).astype(BF16)))
    gc = [dg(tri, g1) + dg(tri, g2) + dg(tri, g3) for g1, g2, g3 in gparts]

    decay = [jnp.exp(jnp.where(row >= col, x - x.T, NEG)) for x in gc]
    egc = [jnp.exp(x) for x in gc]
    kb = [k[h] * beta[h] for h in heads]
    vb = [v[h] * beta[h] for h in heads]
    akk = [_mm1(kb[h], k[h], _NT) for h in heads]
    qk = [_mm1(q[h], k[h], _NT) for h in heads]
    a = [jnp.where(row > col, akk[h] * decay[h], 0.0) for h in heads]
    attn = [qk[h] * decay[h] for h in heads]
    t = _unit_lower_inverse(a, row, col)
    u = [_mm1(t[h], vb[h]) for h in heads]
    w = [_mm1(t[h], kb[h] * egc[h]) for h in heads]

    gl = [x[ch - 1:ch, :] for x in gc]
    st = [st_scr[h] for h in heads]
    w_s = [_mm1(w[h], st[h]) for h in heads]
    q_s = [_mm1(q[h] * egc[h], st[h]) for h in heads]
    v_new = [u[h] - w_s[h] for h in heads]
    a_v = [_mm1(attn[h], v_new[h]) for h in heads]
    kd_t = [(k[h] * jnp.exp(gl[h] - gc[h])).T for h in heads]
    k_v = [_mm1(kd_t[h], v_new[h]) for h in heads]
    for h in heads:
        st_scr[h] = st[h] * jnp.exp(gl[h]) + k_v[h]
        zz = bz_ref[:, sl[h]]
        o_ref[:, sl[h]] = (_rms(q_s[h] + a_v[h], dng_ref[...]) * (zz * _sigmoid(zz))).astype(o_ref.dtype)

    @pl.when(c == pl.num_programs(1) - 1)
    def _():
        s_ref[0] = st_scr[...]


def _delta_prompt(z, conv_w, conv0, s0, a_log, dt_bias, dn_g, *, batch, seq, chunk):
    assert chunk == B_HEAD_DIM
    nc = seq // chunk
    smem = pl.BlockSpec(memory_space=pltpu.SMEM)
    zb = lambda col: pl.BlockSpec((chunk, B_WIDTH), lambda n, c: (n * nc + c, col // B_WIDTH))
    return pl.pallas_call(
        _delta_prompt_kernel,
        grid=(batch, nc),
        in_specs=[
            smem, smem,
            zb(Z_BQ), zb(Z_BK), zb(Z_BV), zb(Z_BZ),
            pl.BlockSpec((chunk, LANES), lambda n, c: (n * nc + c, Z_BA // LANES)),
            pl.BlockSpec(conv_w.shape, lambda n, c: (0, 0)),
            pl.BlockSpec((1, CONV_W - 1, 3 * B_WIDTH), lambda n, c: (n, 0, 0)),
            pl.BlockSpec((1, B_HEADS, B_HEAD_DIM, B_HEAD_DIM), lambda n, c: (n, 0, 0, 0)),
            pl.BlockSpec((1, B_HEAD_DIM), lambda n, c: (0, 0)),
        ],
        out_specs=[
            pl.BlockSpec((chunk, B_WIDTH), lambda n, c: (n * nc + c, 0)),
            pl.BlockSpec((1, B_HEADS, B_HEAD_DIM, B_HEAD_DIM), lambda n, c: (n, 0, 0, 0)),
        ],
        out_shape=[
            jax.ShapeDtypeStruct((batch * seq, B_WIDTH), BF16),
            jax.ShapeDtypeStruct((batch, B_HEADS, B_HEAD_DIM, B_HEAD_DIM), F32),
        ],
        scratch_shapes=[
            pltpu.VMEM((3, chunk + SUBLANES, B_WIDTH), F32),
            pltpu.VMEM((B_HEADS, B_HEAD_DIM, B_HEAD_DIM), F32),
        ],
        compiler_params=_cparams(("parallel", "arbitrary")),
        name="delta_prompt",
    )(a_log, dt_bias, z, z, z, z, z, conv_w, conv0, s0, dn_g)


def _delta_sample_kernel(alog_ref, dtb_ref, bq_ref, bk_ref, bv_ref, bz_ref, ba_ref,
                         wq_ref, wk_ref, wv_ref, cq_ref, ck_ref, cv_ref, s0_ref, dng_ref,
                         o_ref, s_ref, *, new_rows):
    h = pl.program_id(1)
    d = B_HEAD_DIM

    def conv_rows(x_ref, c_ref, w_ref):
        x = x_ref[0]
        cb = c_ref[0]
        ext = [cb[i:i + 1, :] for i in range(CONV_W - 1)] + [x[t:t + 1, :] for t in range(new_rows)]
        out = []
        for t in range(new_rows):
            acc = ext[t] * w_ref[0:1, :]
            for j in range(1, CONV_W):
                acc = acc + ext[t + j] * w_ref[j:j + 1, :]
            out.append(acc * _sigmoid(acc))
        return out

    qs = conv_rows(bq_ref, cq_ref, wq_ref)
    ks = conv_rows(bk_ref, ck_ref, wk_ref)
    vs = conv_rows(bv_ref, cv_ref, wv_ref)
    ba = ba_ref[0]
    z = bz_ref[0]
    beta_all = _sigmoid(_lane_pick(ba, h))
    g_all = -jnp.exp(jnp.full((1, 1), alog_ref[h], F32)) * _softplus(_lane_pick(ba, B_HEADS + h) + dtb_ref[h])
    eye = lax.broadcasted_iota(jnp.int32, (d, d), 0) == lax.broadcasted_iota(jnp.int32, (d, d), 1)
    to_col = lambda r: jnp.sum(jnp.where(eye, r, 0.0), axis=-1, keepdims=True)

    st = s0_ref[0, 0]
    o_ref[0] = jnp.zeros(o_ref.shape[1:], F32)
    for t in range(new_rows):
        q = qs[t] * lax.rsqrt(jnp.sum(qs[t] * qs[t], axis=-1, keepdims=True) + EPS) * (d ** -0.5)
        k = ks[t] * lax.rsqrt(jnp.sum(ks[t] * ks[t], axis=-1, keepdims=True) + EPS)
        a = jnp.exp(g_all[t:t + 1, :])
        b = beta_all[t:t + 1, :]
        kc = to_col(k)
        k_s = jnp.sum(kc * st, axis=0, keepdims=True)
        st = a * st + kc * (b * (vs[t] - a * k_s))
        o = jnp.sum(to_col(q) * st, axis=0, keepdims=True)
        zz = z[t:t + 1, :]
        o_ref[0, t:t + 1, :] = _rms(o, dng_ref[...]) * (zz * _sigmoid(zz))
    s_ref[0, 0] = st


def _delta_sample(z3, conv_w, conv_buf, s0, a_log, dt_bias, dn_g, *, layer, new_rows):
    n, rows, _ = z3.shape
    smem = pl.BlockSpec(memory_space=pltpu.SMEM)
    zb = lambda col: pl.BlockSpec((1, rows, LANES), lambda b, h: (b, 0, col // LANES + h))
    wb = lambda i: pl.BlockSpec((CONV_W, LANES), lambda b, h: (0, i * B_HEADS + h))
    cb = lambda i: pl.BlockSpec((None, 1, CONV_W - 1, LANES), lambda b, h: (layer, b, 0, i * B_HEADS + h))
    kern = functools.partial(_delta_sample_kernel, new_rows=new_rows)
    return pl.pallas_call(
        kern,
        grid=(n, B_HEADS),
        in_specs=[
            smem, smem,
            zb(Z_BQ), zb(Z_BK), zb(Z_BV), zb(Z_BZ),
            pl.BlockSpec((1, rows, LANES), lambda b, h: (b, 0, Z_BA // LANES)),
            wb(0), wb(1), wb(2), cb(0), cb(1), cb(2),
            pl.BlockSpec((None, 1, 1, B_HEAD_DIM, B_HEAD_DIM), lambda b, h: (layer, b, h, 0, 0)),
            pl.BlockSpec((1, B_HEAD_DIM), lambda b, h: (0, 0)),
        ],
        out_specs=[
            pl.BlockSpec((1, rows, LANES), lambda b, h: (b, 0, h)),
            pl.BlockSpec((1, 1, B_HEAD_DIM, B_HEAD_DIM), lambda b, h: (b, h, 0, 0)),
        ],
        out_shape=[
            jax.ShapeDtypeStruct((n, rows, B_WIDTH), F32),
            jax.ShapeDtypeStruct((n, B_HEADS, B_HEAD_DIM, B_HEAD_DIM), F32),
        ],
        compiler_params=_cparams(("parallel", "arbitrary")),
        name="delta_sample",
    )(a_log, dt_bias, z3, z3, z3, z3, z3, conv_w, conv_w, conv_w, conv_buf, conv_buf, conv_buf, s0, dn_g)


def _layout_w_in(w_in):
    order = np.asarray(C_HEAD_ORDER)
    cq = w_in[:, :, O_CQ:O_CKV].reshape(DEPTH, D_MODEL, C_HEADS, HEAD_DIM)[:, :, order]
    cq = cq.reshape(DEPTH, D_MODEL, C_WIDTH)
    pad = jnp.zeros((DEPTH, D_MODEL, Z_COLS - Z_BA - 2 * B_HEADS), w_in.dtype)
    parts = [w_in[:, :, :O_BQKV], cq, w_in[:, :, O_BQKV:O_BETA], w_in[:, :, O_CKV:O_END],
             w_in[:, :, O_BETA:O_CQ], pad]
    return jnp.concatenate(parts, axis=-1).astype(BF16)


def _layout_w_out(w_out):
    order = np.asarray(C_HEAD_ORDER)
    c_rows = w_out[:, A_WIDTH + B_WIDTH:].reshape(DEPTH, C_HEADS, HEAD_DIM, D_MODEL)[:, order]
    c_rows = c_rows.reshape(DEPTH, C_WIDTH, D_MODEL)
    return jnp.concatenate([w_out[:, :A_WIDTH + B_WIDTH], c_rows], axis=1).astype(BF16)


def _pair_table(v):
    return jnp.asarray(v, F32).reshape(-1, 2)


def _feature_major(cache):
    depth, n, tokens = cache.shape[:3]
    return jnp.transpose(cache, (0, 1, 3, 4, 5, 2)).reshape(depth, n, -1, tokens)


def kernel(x_prompt, x_sample, cache_dilated_kv, cache_swa_kv, state_delta_s, state_delta_conv, g_pre_mix,
           w_in, delta_conv_w, delta_a_log, delta_dt_bias, delta_norm_g, swa_sinks, w_out, g_post_mix,
           g_pre_mlp, w_up, w_down, g_post_mlp):
    batch, seq, _ = x_prompt.shape
    dec_batch, dec_seq, _ = x_sample.shape
    past_a = cache_dilated_kv.shape[2]
    past_c = cache_swa_kv.shape[2]
    assert seq % (TQ * A_CLASS_BLOCKS) == 0 and seq == A_WINDOW_MAX and dec_seq <= SAMPLE_ROWS
    assert past_a == A_WINDOW_MAX and past_c == C_WINDOW

    w_in_p = _layout_w_in(w_in)
    w_out_p = _layout_w_out(w_out)
    w_up_b = w_up.astype(BF16)
    w_dn_b = w_down.astype(BF16)
    cache_a = _feature_major(cache_dilated_kv)
    cache_c = _feature_major(cache_swa_kv)

    tab_pa = _prompt_tables_a(seq)
    tab_pc = _prompt_tables_c()
    tab_sa = _sample_tables(_log_mult_a, past_a, SAMPLE_ROWS)
    tab_sc = _sample_tables(_log_mult_c, past_c, SAMPLE_ROWS)
    slopes_a = _pair_table(_alibi(A_HEADS))
    slopes_c = _pair_table(_alibi(C_HEADS)[np.asarray(C_HEAD_ORDER)])
    no_sinks = jnp.zeros_like(slopes_a)
    c_order = np.asarray(C_HEAD_ORDER)

    conv0 = jnp.zeros((batch, CONV_W - 1, 3 * B_WIDTH), F32)
    s_zero = jnp.zeros((batch, B_HEADS, B_HEAD_DIM, B_HEAD_DIM), F32)

    xp = x_prompt.reshape(batch * seq, D_MODEL)
    xs = jnp.pad(x_sample, ((0, 0), (0, SAMPLE_ROWS - dec_seq), (0, 0))).reshape(dec_batch * SAMPLE_ROWS, D_MODEL)
    ms = xs.shape[0]

    p_akv, p_ckv, p_s, p_conv = [], [], [], []
    s_akv, s_ckv, s_s, s_conv = [], [], [], []
    for l in range(DEPTH):
        row = lambda a: a[l].reshape(1, -1)
        sinks_c = _pair_table(swa_sinks[l][c_order])
        zp = _inproj(xp, row(g_pre_mix), w_in_p, l, 512, 1920)
        zs = _inproj(xs, row(g_pre_mix), w_in_p, l, ms, 1920)
        zs3 = zs.reshape(dec_batch, SAMPLE_ROWS, Z_COLS)

        a_p = _attn_a_prompt(zp, slopes_a, tab_pa, batch=batch, seq=seq)
        c_p = _attn_c_prompt(zp, slopes_c, sinks_c, tab_pc, batch=batch, seq=seq, blocks=4)
        b_p, st_p = _delta_prompt(zp, delta_conv_w[l], conv0, s_zero, delta_a_log[l], delta_dt_bias[l],
                                  row(delta_norm_g), batch=batch, seq=seq, chunk=128)

        a_s = _attn_sample(zs3, cache_a, slopes_a, no_sinks, tab_sa, layer=l, pairs=A_HEADS // 2,
                           q_col=Z_AQ, k_col=Z_AK, v_col=Z_AV, v_blk=A_WIDTH // LANES,
                           kv_of_pair=lambda p: p, new_rows=dec_seq, has_sink=False, name="attn_a_sample")
        c_s = _attn_sample(zs3, cache_c, slopes_c, sinks_c, tab_sc, layer=l, pairs=C_HEADS // 2,
                           q_col=Z_CQ, k_col=Z_CK, v_col=Z_CV, v_blk=C_KV_WIDTH // LANES,
                           kv_of_pair=lambda p: p // 3, new_rows=dec_seq, has_sink=True, name="attn_c_sample")
        b_s, st_s = _delta_sample(zs3, delta_conv_w[l], state_delta_conv, state_delta_s, delta_a_log[l],
                                  delta_dt_bias[l], row(delta_norm_g), layer=l, new_rows=dec_seq)

        mlp_w = (w_out_p, row(g_post_mix), row(g_pre_mlp), w_up_b, w_dn_b, row(g_post_mlp))
        xp = _mlp(a_p, b_p, c_p, xp, *mlp_w, l, 512, 512)
        xs = _mlp(a_s.reshape(ms, A_WIDTH), b_s.reshape(ms, B_WIDTH), c_s.reshape(ms, C_WIDTH), xs,
                  *mlp_w, l, ms, 512)

        zp3 = zp.reshape(batch, seq, Z_COLS)
        p_akv.append(zp3[:, seq - A_WINDOW_MAX:, Z_AK:Z_AK + 2 * A_WIDTH]
                     .reshape(batch, A_WINDOW_MAX, 2, A_HEADS, HEAD_DIM))
        p_ckv.append(zp3[:, seq - C_WINDOW:, Z_CK:Z_CK + 2 * C_KV_WIDTH]
                     .reshape(batch, C_WINDOW, 2, C_KV_HEADS, HEAD_DIM))
        p_s.append(st_p)
        p_conv.append(zp3[:, seq - (CONV_W - 1):, Z_BQ:Z_BQ + 3 * B_WIDTH])
        s_akv.append(zs3[:, :dec_seq, Z_AK:Z_AK + 2 * A_WIDTH].reshape(dec_batch, dec_seq, 2, A_HEADS, HEAD_DIM))
        s_ckv.append(zs3[:, :dec_seq, Z_CK:Z_CK + 2 * C_KV_WIDTH]
                     .reshape(dec_batch, dec_seq, 2, C_KV_HEADS, HEAD_DIM))
        s_s.append(st_s)
        full = jnp.concatenate([state_delta_conv[l], zs3[:, :dec_seq, Z_BQ:Z_BQ + 3 * B_WIDTH]], axis=1)
        s_conv.append(full[:, full.shape[1] - (CONV_W - 1):])

    yp = xp.reshape(batch, seq, D_MODEL)
    ys = xs.reshape(dec_batch, SAMPLE_ROWS, D_MODEL)[:, :dec_seq]
    return (yp, ys, jnp.stack(p_akv), jnp.stack(p_ckv), jnp.stack(p_s), jnp.stack(p_conv),
            jnp.stack(s_akv), jnp.stack(s_ckv), jnp.stack(s_s), jnp.stack(s_conv))
```

```python
import functools

import numpy as np
import jax
import jax.numpy as jnp
from jax import lax
from jax.experimental import pallas as pl
from jax.experimental.pallas import tpu as pltpu

F32 = jnp.float32
BF16 = jnp.bfloat16

D_MODEL = 2048
DEPTH = 4
HEAD_DIM = 64
A_HEADS = 12
A_BRANCHES = ((128, 1), (512, 4), (2048, 16))
A_WINDOW_MAX = 2048
B_HEAD_DIM = 128
B_HEADS = 4
CONV_W = 4
C_HEADS = 12
C_KV_HEADS = 4
C_WINDOW = 128
A_WIDTH = A_HEADS * HEAD_DIM
B_WIDTH = B_HEADS * B_HEAD_DIM
C_WIDTH = C_HEADS * HEAD_DIM
C_KV_WIDTH = C_KV_HEADS * HEAD_DIM
D_FF = 4 * D_MODEL
EPS = 1e-6
ATTN_SCALE = HEAD_DIM ** -0.5
NEG = -1e30

LANES = 128
SUBLANES = 8
VMEM_LIMIT = 56 * 1024 * 1024

Z_AQ, Z_AK, Z_AV = 0, 768, 1536
Z_CQ = 2304
Z_BQ, Z_BK, Z_BV, Z_BZ = 3072, 3584, 4096, 4608
Z_CK, Z_CV = 5120, 5376
Z_BA = 5632
Z_COLS = 5760
O_BQKV, O_BZ, O_BETA, O_CQ, O_CKV, O_END = 2304, 3840, 4352, 4360, 5128, 5640
C_HEAD_ORDER = (0, 3, 1, 4, 2, 5, 6, 9, 7, 10, 8, 11)
SAMPLE_ROWS = 8
TQ = 128
A_CLASS_BLOCKS = 4

_NN = (((1,), (0,)), ((), ()))
_NT = (((1,), (1,)), ((), ()))


def _alibi(n):
    return np.asarray([2.0 ** (-8.0 * (i + 1) / n) for i in range(n)], dtype=np.float32)


def _cparams(sem):
    return pltpu.CompilerParams(dimension_semantics=sem, vmem_limit_bytes=VMEM_LIMIT)


def _rms(x, g):
    return x * lax.rsqrt(jnp.mean(x * x, axis=-1, keepdims=True) + EPS) * g


def _sigmoid(x):
    return 1.0 / (1.0 + jnp.exp(-x))


def _inproj_kernel(x_ref, g_ref, w_ref, z_ref, h_scr):
    @pl.when(pl.program_id(1) == 0)
    def _():
        h_scr[...] = _rms(x_ref[...], g_ref[...]).astype(BF16)

    z_ref[...] = jnp.dot(h_scr[...], w_ref[...], preferred_element_type=F32)


def _inproj(x, g, w_all, layer, tm, tn):
    m = x.shape[0]
    return pl.pallas_call(
        _inproj_kernel,
        grid=(m // tm, Z_COLS // tn),
        in_specs=[
            pl.BlockSpec((tm, D_MODEL), lambda i, j: (i, 0)),
            pl.BlockSpec((1, D_MODEL), lambda i, j: (0, 0)),
            pl.BlockSpec((None, D_MODEL, tn), lambda i, j: (layer, 0, j)),
        ],
        out_specs=pl.BlockSpec((tm, tn), lambda i, j: (i, j)),
        out_shape=jax.ShapeDtypeStruct((m, Z_COLS), F32),
        scratch_shapes=[pltpu.VMEM((tm, D_MODEL), BF16)],
        compiler_params=_cparams(("parallel", "arbitrary")),
        name="inproj",
    )(x, g, w_all)


def _mlp_kernel(ma_ref, mb_ref, mc_ref, x_ref, wo_ref, gpm_ref, gpre_ref, wup_ref, wdn_ref, gpost_ref,
                y_ref, hm_scr, acc_scr):
    f = pl.program_id(1)

    @pl.when(f == 0)
    def _():
        y = jnp.dot(ma_ref[...].astype(BF16), wo_ref[0:A_WIDTH, :], preferred_element_type=F32)
        y += jnp.dot(mb_ref[...].astype(BF16), wo_ref[A_WIDTH:A_WIDTH + B_WIDTH, :],
                     preferred_element_type=F32)
        y += jnp.dot(mc_ref[...].astype(BF16), wo_ref[A_WIDTH + B_WIDTH:, :], preferred_element_type=F32)
        x1 = x_ref[...] + _rms(y, gpm_ref[...])
        y_ref[...] = x1
        hm_scr[...] = _rms(x1, gpre_ref[...]).astype(BF16)
        acc_scr[...] = jnp.zeros_like(acc_scr)

    u = jnp.dot(hm_scr[...], wup_ref[...], preferred_element_type=F32)
    u = jnp.square(jnp.maximum(u, 0.0)).astype(BF16)
    acc_scr[...] += jnp.dot(u, wdn_ref[...], preferred_element_type=F32)

    @pl.when(f == pl.num_programs(1) - 1)
    def _():
        y_ref[...] += _rms(acc_scr[...], gpost_ref[...])


def _mlp(ma, mb, mc, x, wo_all, gpm, gpre, wup_all, wdn_all, gpost, layer, tm, tf):
    m = x.shape[0]
    row = lambda i, f: (i, 0)
    vec = pl.BlockSpec((1, D_MODEL), lambda i, f: (0, 0))
    return pl.pallas_call(
        _mlp_kernel,
        grid=(m // tm, D_FF // tf),
        in_specs=[
            pl.BlockSpec((tm, A_WIDTH), row),
            pl.BlockSpec((tm, B_WIDTH), row),
            pl.BlockSpec((tm, C_WIDTH), row),
            pl.BlockSpec((tm, D_MODEL), row),
            pl.BlockSpec((None, D_MODEL, D_MODEL), lambda i, f: (layer, 0, 0)),
            vec, vec,
            pl.BlockSpec((None, D_MODEL, tf), lambda i, f: (layer, 0, f)),
            pl.BlockSpec((None, tf, D_MODEL), lambda i, f: (layer, f, 0)),
            vec,
        ],
        out_specs=pl.BlockSpec((tm, D_MODEL), row),
        out_shape=jax.ShapeDtypeStruct((m, D_MODEL), F32),
        scratch_shapes=[pltpu.VMEM((tm, D_MODEL), BF16), pltpu.VMEM((tm, D_MODEL), F32)],
        compiler_params=_cparams(("parallel", "arbitrary")),
        name="outproj_mlp",
    )(ma, mb, mc, x, wo_all, gpm, gpre, wup_all, wdn_all, gpost)


def _log_mult_a(d):
    c = np.zeros(d.shape, np.int64)
    for window, dil in A_BRANCHES:
        c += ((d >= 0) & (d <= window) & (d % dil == 0)).astype(np.int64)
    return np.where(c > 0, np.log(np.maximum(c, 1)), NEG).astype(np.float32)


def _log_mult_c(d):
    return np.where((d >= 0) & (d <= C_WINDOW), 0.0, NEG).astype(np.float32)


def _tables(log_mult, d):
    return jnp.asarray(np.maximum(d, 0).astype(np.float32)), jnp.asarray(log_mult(d))


def _prompt_tables_a(seq):
    d = np.arange(TQ)[:, None] - np.arange(seq)[None, :] + (seq - TQ)
    return _tables(_log_mult_a, d)


def _prompt_tables_c():
    i = np.arange(TQ)[:, None]
    j = np.arange(2 * TQ)[None, :]
    return _tables(_log_mult_c, np.stack([TQ + i - j, i - j]))


def _sample_tables(log_mult, past, rows):
    d_c = past + np.arange(rows)[:, None] - np.arange(past)[None, :]
    d_n = np.arange(rows)[:, None] - np.arange(LANES)[None, :]
    d_n = np.where(np.arange(LANES)[None, :] < rows, d_n, -1)
    return _tables(log_mult, d_c) + _tables(log_mult, d_n)


def _head_masks(rows):
    left = lax.broadcasted_iota(jnp.int32, (rows, LANES), 1) < HEAD_DIM
    return left, (lambda x: jnp.where(left, x, 0.0), lambda x: jnp.where(left, 0.0, x))


def _stack_heads(q, pick):
    return jnp.concatenate([pick[0](q), pick[1](q)], axis=0)


def _softmax_pv(s, v, extra=None):
    m = jnp.max(s, axis=-1, keepdims=True)
    pr = jnp.exp(s - m)
    l = jnp.sum(pr, axis=-1, keepdims=True)
    if extra is not None:
        l = l + jnp.exp(extra - m)
    return jnp.dot(pr.astype(BF16), v, preferred_element_type=F32) / l


def _attn_a_prompt_kernel(slope_ref, q_ref, k_ref, v_ref, d_ref, l_ref, o_ref, bias_scr, k_scr, v_scr, *,
                          n_cls):
    p = pl.program_id(1)
    c = pl.program_id(2)
    seq = k_ref.shape[0]
    left, pick = _head_masks(TQ)

    @pl.when(c == 0)
    def _():
        for h in range(2):
            bias_scr[h * TQ:(h + 1) * TQ, :] = l_ref[...] - slope_ref[p, h] * d_ref[...]
        k_scr[...] = k_ref[...].astype(BF16)
        v_scr[...] = v_ref[...].astype(BF16)

    for cc in range(n_cls):
        @pl.when(c == cc)
        def _(cc=cc):
            blocks = []
            for j in range(A_CLASS_BLOCKS):
                keys = TQ * (cc * A_CLASS_BLOCKS + j + 1)
                off = seq - keys
                qq = _stack_heads(q_ref[j * TQ:(j + 1) * TQ, :] * ATTN_SCALE, pick).astype(BF16)
                s = lax.dot_general(qq, k_scr[0:keys, :], _NT, preferred_element_type=F32)
                blocks.append((j, keys, s + bias_scr[:, off:off + keys]))
            for j, keys, s in blocks:
                o = _softmax_pv(s, v_scr[0:keys, :])
                o_ref[j * TQ:(j + 1) * TQ, :] = jnp.where(left, o[:TQ], o[TQ:]).astype(o_ref.dtype)


def _attn_a_prompt(z, slopes, tables, *, batch, seq):
    pairs = A_HEADS // 2
    rows = TQ * A_CLASS_BLOCKS
    n_cls = seq // rows
    dtab, ltab = tables
    const2 = lambda b, p, c: (0, 0)
    kern = functools.partial(_attn_a_prompt_kernel, n_cls=n_cls)
    return pl.pallas_call(
        kern,
        grid=(batch, pairs, n_cls),
        in_specs=[
            pl.BlockSpec(memory_space=pltpu.SMEM),
            pl.BlockSpec((rows, LANES), lambda b, p, c: (b * n_cls + c, Z_AQ // LANES + p)),
            pl.BlockSpec((seq, LANES), lambda b, p, c: (b, Z_AK // LANES + p)),
            pl.BlockSpec((seq, LANES), lambda b, p, c: (b, Z_AV // LANES + p)),
            pl.BlockSpec(dtab.shape, const2),
            pl.BlockSpec(ltab.shape, const2),
        ],
        out_specs=pl.BlockSpec((rows, LANES), lambda b, p, c: (b * n_cls + c, p)),
        out_shape=jax.ShapeDtypeStruct((batch * seq, A_WIDTH), BF16),
        scratch_shapes=[pltpu.VMEM((2 * TQ, seq), F32), pltpu.VMEM((seq, LANES), BF16),
                        pltpu.VMEM((seq, LANES), BF16)],
        compiler_params=_cparams(("parallel", "parallel", "arbitrary")),
        name="attn_a_prompt",
    )(slopes, z, z, z, dtab, ltab)


def _attn_c_prompt_kernel(slope_ref, sink_ref, q_ref, k_ref, v_ref, d_ref, l_ref, o_ref, *, blocks):
    p = pl.program_id(1)
    c = pl.program_id(2)
    base = c * (blocks * TQ)
    left, pick = _head_masks(TQ)
    first = c == 0
    bias = [jnp.concatenate([l_ref[t] - slope_ref[p, h] * d_ref[t] for h in range(2)], axis=0)
            for t in range(2)]
    upper = lax.broadcasted_iota(jnp.int32, (2 * TQ, 1), 0) < TQ
    sink = jnp.where(upper, sink_ref[p, 0], sink_ref[p, 1])
    work = []
    for j in range(blocks):
        start = jnp.maximum(base - TQ, 0) if j == 0 else base + (j - 1) * TQ
        start = pl.multiple_of(start, TQ)
        qq = _stack_heads(q_ref[j * TQ:(j + 1) * TQ, :] * ATTN_SCALE, pick).astype(BF16)
        k = k_ref[pl.ds(start, 2 * TQ), :].astype(BF16)
        b = jnp.where(first, bias[1], bias[0]) if j == 0 else bias[0]
        work.append((j, start, lax.dot_general(qq, k, _NT, preferred_element_type=F32) + b))
    for j, start, s in work:
        o = _softmax_pv(s, v_ref[pl.ds(start, 2 * TQ), :].astype(BF16), extra=sink)
        o_ref[j * TQ:(j + 1) * TQ, :] = jnp.where(left, o[:TQ], o[TQ:]).astype(o_ref.dtype)


def _attn_c_prompt(z, slopes, sinks, tables, *, batch, seq, blocks):
    pairs = C_HEADS // 2
    rows = TQ * blocks
    nchunk = seq // rows
    dtab, ltab = tables
    smem = pl.BlockSpec(memory_space=pltpu.SMEM)
    const3 = lambda b, p, c: (0, 0, 0)
    kern = functools.partial(_attn_c_prompt_kernel, blocks=blocks)
    return pl.pallas_call(
        kern,
        grid=(batch, pairs, nchunk),
        in_specs=[
            smem, smem,
            pl.BlockSpec((rows, LANES), lambda b, p, c: (b * nchunk + c, Z_CQ // LANES + p)),
            pl.BlockSpec((seq, LANES), lambda b, p, c: (b, Z_CK // LANES + p // 3)),
            pl.BlockSpec((seq, LANES), lambda b, p, c: (b, Z_CV // LANES + p // 3)),
            pl.BlockSpec(dtab.shape, const3),
            pl.BlockSpec(ltab.shape, const3),
        ],
        out_specs=pl.BlockSpec((rows, LANES), lambda b, p, c: (b * nchunk + c, p)),
        out_shape=jax.ShapeDtypeStruct((batch * seq, C_WIDTH), BF16),
        compiler_params=_cparams(("parallel", "parallel", "arbitrary")),
        name="attn_c_prompt",
    )(slopes, sinks, z, z, z, dtab, ltab)


def _attn_sample_kernel(sink_ref, q_ref, kn_ref, vn_ref, c_ref, dc_ref, lc_ref, dn_ref, ln_ref, o_ref, *,
                        slopes, kv_of_pair, new_rows, has_sink):
    rows = q_ref.shape[1]
    pairs = len(slopes) // 2
    v_row = c_ref.shape[1] // 2
    left, pick = _head_masks(rows)
    upper = lax.broadcasted_iota(jnp.int32, (2 * rows, 1), 0) < rows
    stacked_bias = lambda p, l_ref, d_ref: jnp.concatenate(
        [l_ref[...] - slopes[2 * p + h] * d_ref[...] for h in range(2)], axis=0)
    feat = lambda p: slice(kv_of_pair(p) * LANES, (kv_of_pair(p) + 1) * LANES)

    work = []
    for p in range(pairs):
        qq = _stack_heads(q_ref[0, :, p * LANES:(p + 1) * LANES] * ATTN_SCALE, pick)
        kc = c_ref[0, feat(p), :].astype(BF16)
        s_c = jnp.dot(qq.astype(BF16), kc, preferred_element_type=F32) + stacked_bias(p, lc_ref, dc_ref)
        work.append((qq, s_c))
    for p, (qq, s_c) in enumerate(work):
        kn = kn_ref[0, :, feat(p)]
        vn = vn_ref[0, :, feat(p)]
        bias_n = stacked_bias(p, ln_ref, dn_ref)
        s_n = [jnp.sum(qq * kn[j:j + 1, :], axis=-1, keepdims=True) + bias_n[:, j:j + 1]
               for j in range(new_rows)]
        m = jnp.max(s_c, axis=-1, keepdims=True)
        for s in s_n:
            m = jnp.maximum(m, s)
        p_c = jnp.exp(s_c - m)
        l = jnp.sum(p_c, axis=-1, keepdims=True)
        vc = c_ref[0, v_row + kv_of_pair(p) * LANES:v_row + (kv_of_pair(p) + 1) * LANES, :].astype(BF16)
        o = lax.dot_general(p_c.astype(BF16), vc, _NT, preferred_element_type=F32)
        for j, s in enumerate(s_n):
            p_n = jnp.exp(s - m)
            l = l + p_n
            o = o + p_n * vn[j:j + 1, :]
        if has_sink:
            l = l + jnp.exp(jnp.where(upper, sink_ref[p, 0], sink_ref[p, 1]) - m)
        o = o / l
        o_ref[0, :, p * LANES:(p + 1) * LANES] = jnp.where(left, o[:rows], o[rows:])


def _attn_sample(z3, cache_t, slopes, sinks, tables, *, layer, q_col, k_col, v_col, kv_width,
                 kv_of_pair, new_rows, has_sink, name):
    n, rows, _ = z3.shape
    feats, past = cache_t.shape[2:]
    width = len(slopes) // 2 * LANES
    const2 = lambda b: (0, 0)
    zblk = lambda col, w: pl.BlockSpec((1, rows, w), lambda b: (b, 0, col // w))
    kern = functools.partial(_attn_sample_kernel, slopes=tuple(float(s) for s in slopes),
                             kv_of_pair=kv_of_pair, new_rows=new_rows, has_sink=has_sink)
    return pl.pallas_call(
        kern,
        grid=(n,),
        in_specs=[
            pl.BlockSpec(memory_space=pltpu.SMEM),
            zblk(q_col, width), zblk(k_col, kv_width), zblk(v_col, kv_width),
            pl.BlockSpec((None, 1, feats, past), lambda b: (layer, b, 0, 0)),
        ] + [pl.BlockSpec(t.shape, const2) for t in tables],
        out_specs=pl.BlockSpec((1, rows, width), lambda b: (b, 0, 0)),
        out_shape=jax.ShapeDtypeStruct((n, rows, width), F32),
        compiler_params=_cparams(("arbitrary",)),
        name=name,
    )(sinks, z3, z3, z3, cache_t, *tables)


def _split(a):
    hi = a.astype(BF16)
    return hi, (a - hi.astype(F32)).astype(BF16)


def _mm(a, b, dims=_NN):
    dg = lambda x, y: lax.dot_general(x, y, dims, preferred_element_type=F32)
    return dg(a[0], b[0]) + dg(a[0], b[1]) + dg(a[1], b[0])


def _mm1(a, b, dims=_NN):
    return lax.dot_general(a.astype(BF16), b.astype(BF16), dims, preferred_element_type=F32)


def _unit_lower_inverse(mats, row, col):
    blk = lambda s: (row >> s) == (col >> s)
    eye = jnp.where(row == col, 1.0, 0.0)
    b = [jnp.where(blk(4), -a, 0.0) for a in mats]
    t = [eye + x for x in b]
    ps = [_split(x) for x in b]
    for _ in range(3):
        ps = [_split(_mm(x, x)) for x in ps]
        t = [x + _mm(_split(x), y) for x, y in zip(t, ps)]
    for s in (4, 5, 6):
        off = [jnp.where(blk(s + 1), jnp.where(blk(s), 0.0, a), 0.0) for a in mats]
        ts = [_split(x) for x in t]
        mid = [_split(_mm(_split(o), y)) for o, y in zip(off, ts)]
        t = [x - _mm(y, m) for x, y, m in zip(t, ts, mid)]
    return t


def _softplus(x):
    return jnp.maximum(x, 0.0) + jnp.log(1.0 + jnp.exp(-jnp.abs(x)))


def _lane_pick(x, idx):
    lane = lax.broadcasted_iota(jnp.int32, x.shape, 1)
    return jnp.sum(jnp.where(lane == idx, x, 0.0), axis=-1, keepdims=True)


def _delta_prompt_kernel(alog_ref, dtb_ref, bq_ref, bk_ref, bv_ref, bz_ref, ba_ref, cw_ref, cb_ref, s0_ref,
                         dng_ref, o_ref, s_ref, ext_scr, st_scr):
    c = pl.program_id(1)
    ch = bq_ref.shape[0]
    heads = range(B_HEADS)

    @pl.when(c == 0)
    def _():
        ext_scr[:, 0:SUBLANES, :] = jnp.zeros((3, SUBLANES, B_WIDTH), F32)
        for i in range(3):
            ext_scr[i, SUBLANES - (CONV_W - 1):SUBLANES, :] = cb_ref[0, :, i * B_WIDTH:(i + 1) * B_WIDTH]
        st_scr[...] = s0_ref[0]

    conv = []
    for i, ref in enumerate((bq_ref, bk_ref, bv_ref)):
        ext_scr[i, SUBLANES:SUBLANES + ch, :] = ref[...]
        acc = jnp.zeros((ch, B_WIDTH), F32)
        for j in range(CONV_W):
            w = cw_ref[j:j + 1, i * B_WIDTH:(i + 1) * B_WIDTH]
            acc = acc + ext_scr[i, pl.ds(SUBLANES - (CONV_W - 1) + j, ch), :] * w
        ext_scr[i, 0:SUBLANES, :] = ext_scr[i, ch:ch + SUBLANES, :]
        conv.append(acc * _sigmoid(acc))

    row = lax.broadcasted_iota(jnp.int32, (ch, ch), 0)
    col = lax.broadcasted_iota(jnp.int32, (ch, ch), 1)
    tri = jnp.where(row >= col, 1.0, 0.0).astype(BF16)
    ba = ba_ref[...]
    dg = lambda x, y: lax.dot_general(x, y, _NN, preferred_element_type=F32)

    sl = [slice(h * B_HEAD_DIM, (h + 1) * B_HEAD_DIM) for h in heads]
    q = [conv[0][:, s] for s in sl]
    k = [conv[1][:, s] for s in sl]
    v = [conv[2][:, s] for s in sl]
    q = [x * lax.rsqrt(jnp.sum(x * x, axis=-1, keepdims=True) + EPS) * (B_HEAD_DIM ** -0.5) for x in q]
    k = [x * lax.rsqrt(jnp.sum(x * x, axis=-1, keepdims=True) + EPS) for x in k]
    beta = [_sigmoid(_lane_pick(ba, h)) for h in heads]

    gparts = []
    for h in heads:
        g = -jnp.exp(jnp.full((1, 1), alog_ref[h], F32)) * _softplus(_lane_pick(ba, B_HEADS + h) + dtb_ref[h])
        gb = jnp.broadcast_to(g, (ch, ch))
        g1 = gb.astype(BF16)
        r1 = gb - g1.astype(F32)
        g2 = r1.astype(BF16)
        gparts.append((g1, g2, (r1 - g2.astype(F32)).astype(BF16)))
    gc = [dg(tri, g1) + dg(tri, g2) + dg(tri, g3) for g1, g2, g3 in gparts]

    decay = [jnp.exp(jnp.where(row >= col, x - x.T, NEG)) for x in gc]
    egc = [jnp.exp(x) for x in gc]
    kb = [k[h] * beta[h] for h in heads]
    vb = [v[h] * beta[h] for h in heads]
    akk = [_mm1(kb[h], k[h], _NT) for h in heads]
    qk = [_mm1(q[h], k[h], _NT) for h in heads]
    a = [jnp.where(row > col, akk[h] * decay[h], 0.0) for h in heads]
    attn = [qk[h] * decay[h] for h in heads]
    t = _unit_lower_inverse(a, row, col)
    u = [_mm1(t[h], vb[h]) for h in heads]
    w = [_mm1(t[h], kb[h] * egc[h]) for h in heads]

    gl = [x[ch - 1:ch, :] for x in gc]
    st = [st_scr[h] for h in heads]
    w_s = [_mm1(w[h], st[h]) for h in heads]
    q_s = [_mm1(q[h] * egc[h], st[h]) for h in heads]
    v_new = [u[h] - w_s[h] for h in heads]
    a_v = [_mm1(attn[h], v_new[h]) for h in heads]
    kd_t = [(k[h] * jnp.exp(gl[h] - gc[h])).T for h in heads]
    k_v = [_mm1(kd_t[h], v_new[h]) for h in heads]
    for h in heads:
        st_scr[h] = st[h] * jnp.exp(gl[h]) + k_v[h]
        zz = bz_ref[:, sl[h]]
        o_ref[:, sl[h]] = (_rms(q_s[h] + a_v[h], dng_ref[...]) * (zz * _sigmoid(zz))).astype(o_ref.dtype)

    @pl.when(c == pl.num_programs(1) - 1)
    def _():
        s_ref[0] = st_scr[...]


def _delta_prompt(z, conv_w, conv0, s0, a_log, dt_bias, dn_g, *, batch, seq, chunk):
    assert chunk == B_HEAD_DIM
    nc = seq // chunk
    smem = pl.BlockSpec(memory_space=pltpu.SMEM)
    zb = lambda col: pl.BlockSpec((chunk, B_WIDTH), lambda n, c: (n * nc + c, col // B_WIDTH))
    return pl.pallas_call(
        _delta_prompt_kernel,
        grid=(batch, nc),
        in_specs=[
            smem, smem,
            zb(Z_BQ), zb(Z_BK), zb(Z_BV), zb(Z_BZ),
            pl.BlockSpec((chunk, LANES), lambda n, c: (n * nc + c, Z_BA // LANES)),
            pl.BlockSpec(conv_w.shape, lambda n, c: (0, 0)),
            pl.BlockSpec((1, CONV_W - 1, 3 * B_WIDTH), lambda n, c: (n, 0, 0)),
            pl.BlockSpec((1, B_HEADS, B_HEAD_DIM, B_HEAD_DIM), lambda n, c: (n, 0, 0, 0)),
            pl.BlockSpec((1, B_HEAD_DIM), lambda n, c: (0, 0)),
        ],
        out_specs=[
            pl.BlockSpec((chunk, B_WIDTH), lambda n, c: (n * nc + c, 0)),
            pl.BlockSpec((1, B_HEADS, B_HEAD_DIM, B_HEAD_DIM), lambda n, c: (n, 0, 0, 0)),
        ],
        out_shape=[
            jax.ShapeDtypeStruct((batch * seq, B_WIDTH), BF16),
            jax.ShapeDtypeStruct((batch, B_HEADS, B_HEAD_DIM, B_HEAD_DIM), F32),
        ],
        scratch_shapes=[
            pltpu.VMEM((3, chunk + SUBLANES, B_WIDTH), F32),
            pltpu.VMEM((B_HEADS, B_HEAD_DIM, B_HEAD_DIM), F32),
        ],
        compiler_params=_cparams(("parallel", "arbitrary")),
        name="delta_prompt",
    )(a_log, dt_bias, z, z, z, z, z, conv_w, conv0, s0, dn_g)


def _delta_sample_kernel(alog_ref, dtb_ref, bq_ref, bk_ref, bv_ref, bz_ref, ba_ref, cw_ref, cb_ref, s0_ref,
                         dng_ref, o_ref, s_ref, *, new_rows):
    d = B_HEAD_DIM
    heads = range(B_HEADS)
    cb = cb_ref[0]

    def conv_rows(x_ref, i):
        x = x_ref[0]
        seg = slice(i * B_WIDTH, (i + 1) * B_WIDTH)
        ext = [cb[r:r + 1, seg] for r in range(CONV_W - 1)] + [x[t:t + 1, :] for t in range(new_rows)]
        out = []
        for t in range(new_rows):
            acc = ext[t] * cw_ref[0:1, seg]
            for j in range(1, CONV_W):
                acc = acc + ext[t + j] * cw_ref[j:j + 1, seg]
            out.append(acc * _sigmoid(acc))
        return out

    qs, ks, vs = conv_rows(bq_ref, 0), conv_rows(bk_ref, 1), conv_rows(bv_ref, 2)
    ba = ba_ref[0]
    z = bz_ref[0]
    sl = [slice(h * d, (h + 1) * d) for h in heads]
    beta = [_sigmoid(_lane_pick(ba, h)) for h in heads]
    g = [-jnp.exp(jnp.full((1, 1), alog_ref[h], F32)) * _softplus(_lane_pick(ba, B_HEADS + h) + dtb_ref[h])
         for h in heads]
    eye = lax.broadcasted_iota(jnp.int32, (d, d), 0) == lax.broadcasted_iota(jnp.int32, (d, d), 1)
    to_col = lambda r: jnp.sum(jnp.where(eye, r, 0.0), axis=-1, keepdims=True)
    unit = lambda r: r * lax.rsqrt(jnp.sum(r * r, axis=-1, keepdims=True) + EPS)

    st = [s0_ref[0, h] for h in heads]
    o_ref[0] = jnp.zeros(o_ref.shape[1:], F32)
    for t in range(new_rows):
        for h in heads:
            a = jnp.exp(g[h][t:t + 1, :])
            kc = to_col(unit(ks[t][:, sl[h]]))
            k_s = jnp.sum(kc * st[h], axis=0, keepdims=True)
            st[h] = a * st[h] + kc * (beta[h][t:t + 1, :] * (vs[t][:, sl[h]] - a * k_s))
            qc = to_col(unit(qs[t][:, sl[h]]) * (d ** -0.5))
            o = jnp.sum(qc * st[h], axis=0, keepdims=True)
            zz = z[t:t + 1, sl[h]]
            o_ref[0, t:t + 1, sl[h]] = _rms(o, dng_ref[...]) * (zz * _sigmoid(zz))
    for h in heads:
        s_ref[0, h] = st[h]


def _delta_sample(z3, conv_w, conv_buf, s0, a_log, dt_bias, dn_g, *, layer, new_rows):
    n, rows, _ = z3.shape
    smem = pl.BlockSpec(memory_space=pltpu.SMEM)
    zb = lambda col: pl.BlockSpec((1, rows, B_WIDTH), lambda b: (b, 0, col // B_WIDTH))
    state = (B_HEADS, B_HEAD_DIM, B_HEAD_DIM)
    kern = functools.partial(_delta_sample_kernel, new_rows=new_rows)
    return pl.pallas_call(
        kern,
        grid=(n,),
        in_specs=[
            smem, smem,
            zb(Z_BQ), zb(Z_BK), zb(Z_BV), zb(Z_BZ),
            pl.BlockSpec((1, rows, LANES), lambda b: (b, 0, Z_BA // LANES)),
            pl.BlockSpec(conv_w.shape, lambda b: (0, 0)),
            pl.BlockSpec((None, 1, CONV_W - 1, 3 * B_WIDTH), lambda b: (layer, b, 0, 0)),
            pl.BlockSpec((None, 1) + state, lambda b: (layer, b, 0, 0, 0)),
            pl.BlockSpec((1, B_HEAD_DIM), lambda b: (0, 0)),
        ],
        out_specs=[
            pl.BlockSpec((1, rows, B_WIDTH), lambda b: (b, 0, 0)),
            pl.BlockSpec((1,) + state, lambda b: (b, 0, 0, 0)),
        ],
        out_shape=[
            jax.ShapeDtypeStruct((n, rows, B_WIDTH), F32),
            jax.ShapeDtypeStruct((n,) + state, F32),
        ],
        compiler_params=_cparams(("arbitrary",)),
        name="delta_sample",
    )(a_log, dt_bias, z3, z3, z3, z3, z3, conv_w, conv_buf, s0, dn_g)


def _layout_w_in(w_in):
    order = np.asarray(C_HEAD_ORDER)
    cq = w_in[:, :, O_CQ:O_CKV].reshape(DEPTH, D_MODEL, C_HEADS, HEAD_DIM)[:, :, order]
    cq = cq.reshape(DEPTH, D_MODEL, C_WIDTH)
    pad = jnp.zeros((DEPTH, D_MODEL, Z_COLS - Z_BA - 2 * B_HEADS), w_in.dtype)
    parts = [w_in[:, :, :O_BQKV], cq, w_in[:, :, O_BQKV:O_BETA], w_in[:, :, O_CKV:O_END],
             w_in[:, :, O_BETA:O_CQ], pad]
    return jnp.concatenate(parts, axis=-1).astype(BF16)


def _layout_w_out(w_out):
    order = np.asarray(C_HEAD_ORDER)
    c_rows = w_out[:, A_WIDTH + B_WIDTH:].reshape(DEPTH, C_HEADS, HEAD_DIM, D_MODEL)[:, order]
    c_rows = c_rows.reshape(DEPTH, C_WIDTH, D_MODEL)
    return jnp.concatenate([w_out[:, :A_WIDTH + B_WIDTH], c_rows], axis=1).astype(BF16)


def _pair_table(v):
    return jnp.asarray(v, F32).reshape(-1, 2)


def _feature_major(cache):
    depth, n, tokens = cache.shape[:3]
    return jnp.transpose(cache, (0, 1, 3, 4, 5, 2)).reshape(depth, n, -1, tokens)


def kernel(x_prompt, x_sample, cache_dilated_kv, cache_swa_kv, state_delta_s, state_delta_conv, g_pre_mix,
           w_in, delta_conv_w, delta_a_log, delta_dt_bias, delta_norm_g, swa_sinks, w_out, g_post_mix,
           g_pre_mlp, w_up, w_down, g_post_mlp):
    batch, seq, _ = x_prompt.shape
    dec_batch, dec_seq, _ = x_sample.shape
    past_a = cache_dilated_kv.shape[2]
    past_c = cache_swa_kv.shape[2]
    assert seq % (TQ * A_CLASS_BLOCKS) == 0 and seq == A_WINDOW_MAX and dec_seq <= SAMPLE_ROWS
    assert past_a == A_WINDOW_MAX and past_c == C_WINDOW

    w_in_p = _layout_w_in(w_in)
    w_out_p = _layout_w_out(w_out)
    w_up_b = w_up.astype(BF16)
    w_dn_b = w_down.astype(BF16)
    cache_a = _feature_major(cache_dilated_kv)
    cache_c = _feature_major(cache_swa_kv)

    tab_pa = _prompt_tables_a(seq)
    tab_pc = _prompt_tables_c()
    tab_sa = _sample_tables(_log_mult_a, past_a, SAMPLE_ROWS)
    tab_sc = _sample_tables(_log_mult_c, past_c, SAMPLE_ROWS)
    alibi_a = _alibi(A_HEADS)
    alibi_c = _alibi(C_HEADS)[np.asarray(C_HEAD_ORDER)]
    slopes_a = _pair_table(alibi_a)
    slopes_c = _pair_table(alibi_c)
    no_sinks = jnp.zeros_like(slopes_a)
    c_order = np.asarray(C_HEAD_ORDER)

    conv0 = jnp.zeros((batch, CONV_W - 1, 3 * B_WIDTH), F32)
    s_zero = jnp.zeros((batch, B_HEADS, B_HEAD_DIM, B_HEAD_DIM), F32)

    xp = x_prompt.reshape(batch * seq, D_MODEL)
    xs = jnp.pad(x_sample, ((0, 0), (0, SAMPLE_ROWS - dec_seq), (0, 0))).reshape(dec_batch * SAMPLE_ROWS, D_MODEL)
    ms = xs.shape[0]

    p_akv, p_ckv, p_s, p_conv = [], [], [], []
    s_akv, s_ckv, s_s, s_conv = [], [], [], []
    for l in range(DEPTH):
        row = lambda a: a[l].reshape(1, -1)
        sinks_c = _pair_table(swa_sinks[l][c_order])
        zp = _inproj(xp, row(g_pre_mix), w_in_p, l, 512, 1920)
        zs = _inproj(xs, row(g_pre_mix), w_in_p, l, ms, 1920)
        zs3 = zs.reshape(dec_batch, SAMPLE_ROWS, Z_COLS)

        a_p = _attn_a_prompt(zp, slopes_a, tab_pa, batch=batch, seq=seq)
        c_p = _attn_c_prompt(zp, slopes_c, sinks_c, tab_pc, batch=batch, seq=seq, blocks=4)
        b_p, st_p = _delta_prompt(zp, delta_conv_w[l], conv0, s_zero, delta_a_log[l], delta_dt_bias[l],
                                  row(delta_norm_g), batch=batch, seq=seq, chunk=128)

        a_s = _attn_sample(zs3, cache_a, alibi_a, no_sinks, tab_sa, layer=l, q_col=Z_AQ, k_col=Z_AK,
                           v_col=Z_AV, kv_width=A_WIDTH, kv_of_pair=lambda p: p, new_rows=dec_seq,
                           has_sink=False, name="attn_a_sample")
        c_s = _attn_sample(zs3, cache_c, alibi_c, sinks_c, tab_sc, layer=l, q_col=Z_CQ, k_col=Z_CK,
                           v_col=Z_CV, kv_width=C_KV_WIDTH, kv_of_pair=lambda p: p // 3, new_rows=dec_seq,
                           has_sink=True, name="attn_c_sample")
        b_s, st_s = _delta_sample(zs3, delta_conv_w[l], state_delta_conv, state_delta_s, delta_a_log[l],
                                  delta_dt_bias[l], row(delta_norm_g), layer=l, new_rows=dec_seq)

        mlp_w = (w_out_p, row(g_post_mix), row(g_pre_mlp), w_up_b, w_dn_b, row(g_post_mlp))
        xp = _mlp(a_p, b_p, c_p, xp, *mlp_w, l, 512, 512)
        xs = _mlp(a_s.reshape(ms, A_WIDTH), b_s.reshape(ms, B_WIDTH), c_s.reshape(ms, C_WIDTH), xs,
                  *mlp_w, l, ms, 512)

        zp3 = zp.reshape(batch, seq, Z_COLS)
        p_akv.append(zp3[:, seq - A_WINDOW_MAX:, Z_AK:Z_AK + 2 * A_WIDTH]
                     .reshape(batch, A_WINDOW_MAX, 2, A_HEADS, HEAD_DIM))
        p_ckv.append(zp3[:, seq - C_WINDOW:, Z_CK:Z_CK + 2 * C_KV_WIDTH]
                     .reshape(batch, C_WINDOW, 2, C_KV_HEADS, HEAD_DIM))
        p_s.append(st_p)
        p_conv.append(zp3[:, seq - (CONV_W - 1):, Z_BQ:Z_BQ + 3 * B_WIDTH])
        s_akv.append(zs3[:, :dec_seq, Z_AK:Z_AK + 2 * A_WIDTH].reshape(dec_batch, dec_seq, 2, A_HEADS, HEAD_DIM))
        s_ckv.append(zs3[:, :dec_seq, Z_CK:Z_CK + 2 * C_KV_WIDTH]
                     .reshape(dec_batch, dec_seq, 2, C_KV_HEADS, HEAD_DIM))
        s_s.append(st_s)
        full = jnp.concatenate([state_delta_conv[l], zs3[:, :dec_seq, Z_BQ:Z_BQ + 3 * B_WIDTH]], axis=1)
        s_conv.append(full[:, full.shape[1] - (CONV_W - 1):])

    yp = xp.reshape(batch, seq, D_MODEL)
    ys = xs.reshape(dec_batch, SAMPLE_ROWS, D_MODEL)[:, :dec_seq]
    return (yp, ys, jnp.stack(p_akv), jnp.stack(p_ckv), jnp.stack(p_s), jnp.stack(p_conv),
            jnp.stack(s_akv), jnp.stack(s_ckv), jnp.stack(s_s), jnp.stack(s_conv))
```

```python
import functools

import numpy as np
import jax
import jax.numpy as jnp
from jax import lax
from jax.experimental import pallas as pl
from jax.experimental.pallas import tpu as pltpu

F32 = jnp.float32
BF16 = jnp.bfloat16

D_MODEL = 2048
DEPTH = 4
HEAD_DIM = 64
A_HEADS = 12
A_BRANCHES = ((128, 1), (512, 4), (2048, 16))
A_WINDOW_MAX = 2048
B_HEAD_DIM = 128
B_HEADS = 4
CONV_W = 4
C_HEADS = 12
C_KV_HEADS = 4
C_WINDOW = 128
A_WIDTH = A_HEADS * HEAD_DIM
B_WIDTH = B_HEADS * B_HEAD_DIM
C_WIDTH = C_HEADS * HEAD_DIM
C_KV_WIDTH = C_KV_HEADS * HEAD_DIM
D_FF = 4 * D_MODEL
EPS = 1e-6
ATTN_SCALE = HEAD_DIM ** -0.5
LOG2E = 1.4426950408889634
NEG = -1e30

LANES = 128
SUBLANES = 8
VMEM_LIMIT = 56 * 1024 * 1024

Z_TILE = 768
Z_AQ, Z_AK, Z_AV = 0, 768, 1536
Z_CQ = 2304
Z_BQ, Z_BK, Z_BV, Z_BZ = 3072, 3584, 4096, 4608
Z_CK, Z_CV = 5376, 5632
Z_BA = 5888
Z_COLS = 6144
O_BQKV, O_BZ, O_BETA, O_CQ, O_CKV, O_END = 2304, 3840, 4352, 4360, 5128, 5640
C_HEAD_ORDER = (0, 3, 1, 4, 2, 5, 6, 9, 7, 10, 8, 11)
SAMPLE_ROWS = 8
TQ = 128
A_CLASS_BLOCKS = 4

_NN = (((1,), (0,)), ((), ()))
_NT = (((1,), (1,)), ((), ()))


def _alibi(n):
    return np.asarray([2.0 ** (-8.0 * (i + 1) / n) for i in range(n)], dtype=np.float32)


def _cparams(sem):
    return pltpu.CompilerParams(dimension_semantics=sem, vmem_limit_bytes=VMEM_LIMIT)


def _rms(x, g):
    return x * lax.rsqrt(jnp.mean(x * x, axis=-1, keepdims=True) + EPS) * g


def _sigmoid(x):
    return 1.0 / (1.0 + jnp.exp(-x))


def _inproj_kernel(x_ref, g_ref, w_ref, *refs, feature_major):
    j = pl.program_id(1)
    h_scr = refs[-1]
    z_ref = refs[-4] if feature_major else refs[-2]

    @pl.when(j == 0)
    def _():
        h_scr[...] = _rms(x_ref[...], g_ref[...]).astype(BF16)

    z_ref[...] = jnp.dot(h_scr[...], w_ref[...], preferred_element_type=F32)

    if feature_major:
        ta_ref, tc_ref = refs[-3], refs[-2]

        @pl.when((j == Z_AK // Z_TILE) | (j == Z_AV // Z_TILE))
        def _():
            ta_ref[...] = z_ref[...].T

        @pl.when(j == Z_CK // Z_TILE)
        def _():
            tc_ref[...] = z_ref[:, 0:2 * C_KV_WIDTH].T


def _inproj(x, g, w_all, layer, tm, kv_t=None, seq=None):
    m = x.shape[0]
    in_specs = [
        pl.BlockSpec((tm, D_MODEL), lambda i, j: (i, 0)),
        pl.BlockSpec((1, D_MODEL), lambda i, j: (0, 0)),
        pl.BlockSpec((None, D_MODEL, Z_TILE), lambda i, j: (layer, 0, j)),
    ]
    out_specs = [pl.BlockSpec((tm, Z_TILE), lambda i, j: (i, j))]
    out_shape = [jax.ShapeDtypeStruct((m, Z_COLS), F32)]
    args, aliases = [x, g, w_all], {}
    if seq is not None:
        tps = seq // tm
        batch = m // seq
        a_tile = lambda j: jnp.clip(j - Z_AK // Z_TILE, 0, 1)
        out_specs += [
            pl.BlockSpec((None, None, Z_TILE, tm), lambda i, j: (layer, i // tps, a_tile(j), i % tps)),
            pl.BlockSpec((None, None, 2 * C_KV_WIDTH, tm), lambda i, j: (layer, i // tps, 0, i % tps)),
        ]
        out_shape += [jax.ShapeDtypeStruct((DEPTH, batch, 2 * A_WIDTH, seq), F32),
                      jax.ShapeDtypeStruct((DEPTH, batch, 2 * C_KV_WIDTH, seq), F32)]
        if kv_t is not None:
            in_specs += [pl.BlockSpec(memory_space=pl.ANY)] * 2
            aliases = {3: 1, 4: 2}
            args += list(kv_t)
    out = pl.pallas_call(
        functools.partial(_inproj_kernel, feature_major=seq is not None),
        grid=(m // tm, Z_COLS // Z_TILE),
        in_specs=in_specs,
        out_specs=out_specs,
        out_shape=out_shape,
        input_output_aliases=aliases,
        scratch_shapes=[pltpu.VMEM((tm, D_MODEL), BF16)],
        compiler_params=_cparams(("parallel", "arbitrary")),
        name="inproj",
    )(*args)
    return out[0], tuple(out[1:])


def _mlp_kernel(ma_ref, mb_ref, mc_ref, x_ref, wo_ref, gpm_ref, gpre_ref, wup_ref, wdn_ref, gpost_ref,
                y_ref, hm_scr, acc_scr):
    f = pl.program_id(1)

    @pl.when(f == 0)
    def _():
        y = jnp.dot(ma_ref[...].astype(BF16), wo_ref[0:A_WIDTH, :], preferred_element_type=F32)
        y += jnp.dot(mb_ref[...].astype(BF16), wo_ref[A_WIDTH:A_WIDTH + B_WIDTH, :],
                     preferred_element_type=F32)
        y += jnp.dot(mc_ref[...].astype(BF16), wo_ref[A_WIDTH + B_WIDTH:, :], preferred_element_type=F32)
        x1 = x_ref[...] + _rms(y, gpm_ref[...])
        y_ref[...] = x1
        hm_scr[...] = _rms(x1, gpre_ref[...]).astype(BF16)
        acc_scr[...] = jnp.zeros_like(acc_scr)

    u = jnp.dot(hm_scr[...], wup_ref[...], preferred_element_type=F32)
    u = jnp.square(jnp.maximum(u, 0.0)).astype(BF16)
    acc_scr[...] += jnp.dot(u, wdn_ref[...], preferred_element_type=F32)

    @pl.when(f == pl.num_programs(1) - 1)
    def _():
        y_ref[...] += _rms(acc_scr[...], gpost_ref[...])


def _mlp(ma, mb, mc, x, wo_all, gpm, gpre, wup_all, wdn_all, gpost, layer, tm, tf):
    m = x.shape[0]
    row = lambda i, f: (i, 0)
    vec = pl.BlockSpec((1, D_MODEL), lambda i, f: (0, 0))
    return pl.pallas_call(
        _mlp_kernel,
        grid=(m // tm, D_FF // tf),
        in_specs=[
            pl.BlockSpec((tm, A_WIDTH), row),
            pl.BlockSpec((tm, B_WIDTH), row),
            pl.BlockSpec((tm, C_WIDTH), row),
            pl.BlockSpec((tm, D_MODEL), row),
            pl.BlockSpec((None, D_MODEL, D_MODEL), lambda i, f: (layer, 0, 0)),
            vec, vec,
            pl.BlockSpec((None, D_MODEL, tf), lambda i, f: (layer, 0, f)),
            pl.BlockSpec((None, tf, D_MODEL), lambda i, f: (layer, f, 0)),
            vec,
        ],
        out_specs=pl.BlockSpec((tm, D_MODEL), row),
        out_shape=jax.ShapeDtypeStruct((m, D_MODEL), F32),
        scratch_shapes=[pltpu.VMEM((tm, D_MODEL), BF16), pltpu.VMEM((tm, D_MODEL), F32)],
        compiler_params=_cparams(("parallel", "arbitrary")),
        name="outproj_mlp",
    )(ma, mb, mc, x, wo_all, gpm, gpre, wup_all, wdn_all, gpost)


def _log_mult_a(d):
    c = np.zeros(d.shape, np.int64)
    for window, dil in A_BRANCHES:
        c += ((d >= 0) & (d <= window) & (d % dil == 0)).astype(np.int64)
    return np.where(c > 0, np.log(np.maximum(c, 1)), NEG).astype(np.float32)


def _log_mult_c(d):
    return np.where((d >= 0) & (d <= C_WINDOW), 0.0, NEG).astype(np.float32)


def _tables(log_mult, d):
    return jnp.asarray(np.maximum(d, 0).astype(np.float32)), jnp.asarray(log_mult(d))


def _prompt_tables_a(seq):
    d = np.arange(TQ)[:, None] - np.arange(seq)[None, :] + (seq - TQ)
    return _tables(_log_mult_a, d)


def _prompt_tables_c():
    i = np.arange(TQ)[:, None]
    j = np.arange(2 * TQ)[None, :]
    return _tables(_log_mult_c, np.stack([TQ + i - j, i - j]))


def _sample_tables(log_mult, past, rows):
    d_c = past + np.arange(rows)[:, None] - np.arange(past)[None, :]
    d_n = np.arange(rows)[:, None] - np.arange(LANES)[None, :]
    d_n = np.where(np.arange(LANES)[None, :] < rows, d_n, -1)
    return _tables(log_mult, d_c) + _tables(log_mult, d_n)


def _head_masks(rows):
    left = lax.broadcasted_iota(jnp.int32, (rows, LANES), 1) < HEAD_DIM
    return left, (lambda x: jnp.where(left, x, 0.0), lambda x: jnp.where(left, 0.0, x))


def _stack_heads(q, pick):
    return jnp.concatenate([pick[0](q), pick[1](q)], axis=0)


def _softmax_pv(s, v, extra=None, base2=False):
    ex = jnp.exp2 if base2 else jnp.exp
    m = jnp.max(s, axis=-1, keepdims=True)
    pr = ex(s - m)
    l = jnp.sum(pr, axis=-1, keepdims=True)
    if extra is not None:
        l = l + ex(extra - m)
    return jnp.dot(pr.astype(BF16), v, preferred_element_type=F32) / l


def _attn_a_prompt_kernel(slope_ref, q_ref, k_ref, v_ref, d_ref, l_ref, o_ref, bias_scr, k_scr, v_scr, *,
                          n_cls):
    p = pl.program_id(1)
    c = pl.program_id(2)
    seq = k_ref.shape[0]
    left, pick = _head_masks(TQ)

    @pl.when(c == 0)
    def _():
        for h in range(2):
            bias_scr[h * TQ:(h + 1) * TQ, :] = (l_ref[...] - slope_ref[p, h] * d_ref[...]) * LOG2E
        k_scr[...] = k_ref[...].astype(BF16)
        v_scr[...] = v_ref[...].astype(BF16)

    for cc in range(n_cls):
        @pl.when(c == cc)
        def _(cc=cc):
            keys = [TQ * (cc * A_CLASS_BLOCKS + j + 1) for j in range(A_CLASS_BLOCKS)]

            def scores(j):
                qq = _stack_heads(q_ref[j * TQ:(j + 1) * TQ, :] * (ATTN_SCALE * LOG2E), pick).astype(BF16)
                s = lax.dot_general(qq, k_scr[0:keys[j], :], _NT, preferred_element_type=F32)
                return s + bias_scr[:, seq - keys[j]:seq]

            ahead = 2
            s = {j: scores(j) for j in range(min(ahead, A_CLASS_BLOCKS))}
            for j in range(A_CLASS_BLOCKS):
                o = _softmax_pv(s.pop(j), v_scr[0:keys[j], :], base2=True)
                o_ref[j * TQ:(j + 1) * TQ, :] = jnp.where(left, o[:TQ], o[TQ:]).astype(o_ref.dtype)
                if j + ahead < A_CLASS_BLOCKS:
                    s[j + ahead] = scores(j + ahead)


def _attn_a_prompt(z, slopes, tables, *, batch, seq):
    pairs = A_HEADS // 2
    rows = TQ * A_CLASS_BLOCKS
    n_cls = seq // rows
    dtab, ltab = tables
    const2 = lambda b, p, c: (0, 0)
    kern = functools.partial(_attn_a_prompt_kernel, n_cls=n_cls)
    return pl.pallas_call(
        kern,
        grid=(batch, pairs, n_cls),
        in_specs=[
            pl.BlockSpec(memory_space=pltpu.SMEM),
            pl.BlockSpec((rows, LANES), lambda b, p, c: (b * n_cls + c, Z_AQ // LANES + p)),
            pl.BlockSpec((seq, LANES), lambda b, p, c: (b, Z_AK // LANES + p)),
            pl.BlockSpec((seq, LANES), lambda b, p, c: (b, Z_AV // LANES + p)),
            pl.BlockSpec(dtab.shape, const2),
            pl.BlockSpec(ltab.shape, const2),
        ],
        out_specs=pl.BlockSpec((rows, LANES), lambda b, p, c: (b * n_cls + c, p)),
        out_shape=jax.ShapeDtypeStruct((batch * seq, A_WIDTH), BF16),
        scratch_shapes=[pltpu.VMEM((2 * TQ, seq), F32), pltpu.VMEM((seq, LANES), BF16),
                        pltpu.VMEM((seq, LANES), BF16)],
        compiler_params=_cparams(("parallel", "parallel", "arbitrary")),
        name="attn_a_prompt",
    )(slopes, z, z, z, dtab, ltab)


def _attn_c_prompt_kernel(slope_ref, sink_ref, q_ref, k_ref, v_ref, d_ref, l_ref, o_ref, *, blocks):
    p = pl.program_id(1)
    c = pl.program_id(2)
    base = c * (blocks * TQ)
    left, pick = _head_masks(TQ)
    first = c == 0
    bias = [jnp.concatenate([l_ref[t] - slope_ref[p, h] * d_ref[t] for h in range(2)], axis=0)
            for t in range(2)]
    upper = lax.broadcasted_iota(jnp.int32, (2 * TQ, 1), 0) < TQ
    sink = jnp.where(upper, sink_ref[p, 0], sink_ref[p, 1])
    work = []
    for j in range(blocks):
        start = jnp.maximum(base - TQ, 0) if j == 0 else base + (j - 1) * TQ
        start = pl.multiple_of(start, TQ)
        qq = _stack_heads(q_ref[j * TQ:(j + 1) * TQ, :] * ATTN_SCALE, pick).astype(BF16)
        k = k_ref[pl.ds(start, 2 * TQ), :].astype(BF16)
        b = jnp.where(first, bias[1], bias[0]) if j == 0 else bias[0]
        work.append((j, start, lax.dot_general(qq, k, _NT, preferred_element_type=F32) + b))
    for j, start, s in work:
        o = _softmax_pv(s, v_ref[pl.ds(start, 2 * TQ), :].astype(BF16), extra=sink)
        o_ref[j * TQ:(j + 1) * TQ, :] = jnp.where(left, o[:TQ], o[TQ:]).astype(o_ref.dtype)


def _attn_c_prompt(z, slopes, sinks, tables, *, batch, seq, blocks):
    pairs = C_HEADS // 2
    rows = TQ * blocks
    nchunk = seq // rows
    dtab, ltab = tables
    smem = pl.BlockSpec(memory_space=pltpu.SMEM)
    const3 = lambda b, p, c: (0, 0, 0)
    kern = functools.partial(_attn_c_prompt_kernel, blocks=blocks)
    return pl.pallas_call(
        kern,
        grid=(batch, pairs, nchunk),
        in_specs=[
            smem, smem,
            pl.BlockSpec((rows, LANES), lambda b, p, c: (b * nchunk + c, Z_CQ // LANES + p)),
            pl.BlockSpec((seq, LANES), lambda b, p, c: (b, Z_CK // LANES + p // 3)),
            pl.BlockSpec((seq, LANES), lambda b, p, c: (b, Z_CV // LANES + p // 3)),
            pl.BlockSpec(dtab.shape, const3),
            pl.BlockSpec(ltab.shape, const3),
        ],
        out_specs=pl.BlockSpec((rows, LANES), lambda b, p, c: (b * nchunk + c, p)),
        out_shape=jax.ShapeDtypeStruct((batch * seq, C_WIDTH), BF16),
        compiler_params=_cparams(("parallel", "parallel", "arbitrary")),
        name="attn_c_prompt",
    )(slopes, sinks, z, z, z, dtab, ltab)


def _attn_sample_kernel(sink_ref, q_ref, kn_ref, vn_ref, c_ref, dc_ref, lc_ref, dn_ref, ln_ref, o_ref, *,
                        slopes, kv_of_pair, new_rows, has_sink):
    rows = q_ref.shape[1]
    pairs = len(slopes) // 2
    v_row = c_ref.shape[1] // 2
    left, pick = _head_masks(rows)
    upper = lax.broadcasted_iota(jnp.int32, (2 * rows, 1), 0) < rows
    stacked_bias = lambda p, l_ref, d_ref: jnp.concatenate(
        [l_ref[...] - slopes[2 * p + h] * d_ref[...] for h in range(2)], axis=0)
    feat = lambda p: slice(kv_of_pair(p) * LANES, (kv_of_pair(p) + 1) * LANES)

    work = []
    for p in range(pairs):
        qq = _stack_heads(q_ref[0, :, p * LANES:(p + 1) * LANES] * ATTN_SCALE, pick)
        kc = c_ref[0, feat(p), :].astype(BF16)
        s_c = jnp.dot(qq.astype(BF16), kc, preferred_element_type=F32) + stacked_bias(p, lc_ref, dc_ref)
        work.append((qq, s_c))
    for p, (qq, s_c) in enumerate(work):
        kn = kn_ref[0, :, feat(p)]
        vn = vn_ref[0, :, feat(p)]
        bias_n = stacked_bias(p, ln_ref, dn_ref)
        s_n = [jnp.sum(qq * kn[j:j + 1, :], axis=-1, keepdims=True) + bias_n[:, j:j + 1]
               for j in range(new_rows)]
        m = jnp.max(s_c, axis=-1, keepdims=True)
        for s in s_n:
            m = jnp.maximum(m, s)
        p_c = jnp.exp(s_c - m)
        l = jnp.sum(p_c, axis=-1, keepdims=True)
        vc = c_ref[0, v_row + kv_of_pair(p) * LANES:v_row + (kv_of_pair(p) + 1) * LANES, :].astype(BF16)
        o = lax.dot_general(p_c.astype(BF16), vc, _NT, preferred_element_type=F32)
        for j, s in enumerate(s_n):
            p_n = jnp.exp(s - m)
            l = l + p_n
            o = o + p_n * vn[j:j + 1, :]
        if has_sink:
            l = l + jnp.exp(jnp.where(upper, sink_ref[p, 0], sink_ref[p, 1]) - m)
        o = o / l
        o_ref[0, :, p * LANES:(p + 1) * LANES] = jnp.where(left, o[:rows], o[rows:])


def _attn_sample(z3, cache_t, slopes, sinks, tables, *, layer, q_col, k_col, v_col, kv_width,
                 kv_of_pair, new_rows, has_sink, name):
    n, rows, _ = z3.shape
    feats, past = cache_t.shape[2:]
    width = len(slopes) // 2 * LANES
    const2 = lambda b: (0, 0)
    zblk = lambda col, w: pl.BlockSpec((1, rows, w), lambda b: (b, 0, col // w))
    kern = functools.partial(_attn_sample_kernel, slopes=tuple(float(s) for s in slopes),
                             kv_of_pair=kv_of_pair, new_rows=new_rows, has_sink=has_sink)
    return pl.pallas_call(
        kern,
        grid=(n,),
        in_specs=[
            pl.BlockSpec(memory_space=pltpu.SMEM),
            zblk(q_col, width), zblk(k_col, kv_width), zblk(v_col, kv_width),
            pl.BlockSpec((None, 1, feats, past), lambda b: (layer, b, 0, 0)),
        ] + [pl.BlockSpec(t.shape, const2) for t in tables],
        out_specs=pl.BlockSpec((1, rows, width), lambda b: (b, 0, 0)),
        out_shape=jax.ShapeDtypeStruct((n, rows, width), F32),
        compiler_params=_cparams(("arbitrary",)),
        name=name,
    )(sinks, z3, z3, z3, cache_t, *tables)


def _mm1(a, b, dims=_NN):
    return lax.dot_general(a.astype(BF16), b.astype(BF16), dims, preferred_element_type=F32)


def _unit_lower_inverse(mats, row, col):
    blk = lambda s: (row >> s) == (col >> s)
    eye = jnp.where(row == col, 1.0, 0.0)
    pw = [jnp.where(blk(4), -a, 0.0) for a in mats]
    t = [eye + x for x in pw]
    for _ in range(3):
        pw = [_mm1(x, x).astype(BF16) for x in pw]
        t = [x + _mm1(x, y) for x, y in zip(t, pw)]
    for s in (4, 5, 6):
        off = [jnp.where(blk(s + 1), jnp.where(blk(s), 0.0, a), 0.0).astype(BF16) for a in mats]
        tb = [x.astype(BF16) for x in t]
        mid = [_mm1(o, y) for o, y in zip(off, tb)]
        t = [x - _mm1(y, m) for x, y, m in zip(t, tb, mid)]
    return t


def _softplus(x):
    return jnp.maximum(x, 0.0) + jnp.log(1.0 + jnp.exp(-jnp.abs(x)))


def _lane_pick(x, idx):
    lane = lax.broadcasted_iota(jnp.int32, x.shape, 1)
    return jnp.sum(jnp.where(lane == idx, x, 0.0), axis=-1, keepdims=True)


def _delta_prompt_kernel(alog_ref, dtb_ref, bq_ref, bk_ref, bv_ref, bz_ref, ba_ref, cw_ref, cb_ref, s0_ref,
                         dng_ref, o_ref, s_ref, ext_scr, st_scr):
    c = pl.program_id(1)
    ch = bq_ref.shape[0]
    heads = range(B_HEADS)

    @pl.when(c == 0)
    def _():
        ext_scr[:, 0:SUBLANES, :] = jnp.zeros((3, SUBLANES, B_WIDTH), F32)
        for i in range(3):
            ext_scr[i, SUBLANES - (CONV_W - 1):SUBLANES, :] = cb_ref[0, :, i * B_WIDTH:(i + 1) * B_WIDTH]
        st_scr[...] = s0_ref[0]

    conv = []
    for i, ref in enumerate((bq_ref, bk_ref, bv_ref)):
        ext_scr[i, SUBLANES:SUBLANES + ch, :] = ref[...]
        acc = jnp.zeros((ch, B_WIDTH), F32)
        for j in range(CONV_W):
            w = cw_ref[j:j + 1, i * B_WIDTH:(i + 1) * B_WIDTH]
            acc = acc + ext_scr[i, pl.ds(SUBLANES - (CONV_W - 1) + j, ch), :] * w
        ext_scr[i, 0:SUBLANES, :] = ext_scr[i, ch:ch + SUBLANES, :]
        conv.append(acc * _sigmoid(acc))

    row = lax.broadcasted_iota(jnp.int32, (ch, ch), 0)
    col = lax.broadcasted_iota(jnp.int32, (ch, ch), 1)
    tri = jnp.where(row >= col, 1.0, 0.0).astype(BF16)
    ba = ba_ref[...]
    dg = lambda x, y: lax.dot_general(x, y, _NN, preferred_element_type=F32)

    sl = [slice(h * B_HEAD_DIM, (h + 1) * B_HEAD_DIM) for h in heads]
    q = [conv[0][:, s] for s in sl]
    k = [conv[1][:, s] for s in sl]
    v = [conv[2][:, s] for s in sl]
    q = [x * lax.rsqrt(jnp.sum(x * x, axis=-1, keepdims=True) + EPS) * (B_HEAD_DIM ** -0.5) for x in q]
    k = [x * lax.rsqrt(jnp.sum(x * x, axis=-1, keepdims=True) + EPS) for x in k]
    beta = [_sigmoid(_lane_pick(ba, h)) for h in heads]

    gparts = []
    for h in heads:
        g = -jnp.exp(jnp.full((1, 1), alog_ref[h], F32)) * _softplus(_lane_pick(ba, B_HEADS + h) + dtb_ref[h])
        gb = jnp.broadcast_to(g, (ch, ch))
        g1 = gb.astype(BF16)
        r1 = gb - g1.astype(F32)
        g2 = r1.astype(BF16)
        gparts.append((g1, g2, (r1 - g2.astype(F32)).astype(BF16)))
    gc = [dg(tri, g1) + dg(tri, g2) + dg(tri, g3) for g1, g2, g3 in gparts]

    decay = [jnp.exp(jnp.where(row >= col, x - x.T, NEG)) for x in gc]
    egc = [jnp.exp(x) for x in gc]
    kb = [k[h] * beta[h] for h in heads]
    vb = [v[h] * beta[h] for h in heads]
    akk = [_mm1(kb[h], k[h], _NT) for h in heads]
    qk = [_mm1(q[h], k[h], _NT) for h in heads]
    a = [jnp.where(row > col, akk[h] * decay[h], 0.0) for h in heads]
    attn = [qk[h] * decay[h] for h in heads]
    t = _unit_lower_inverse(a, row, col)
    u = [_mm1(t[h], vb[h]) for h in heads]
    w = [_mm1(t[h], kb[h] * egc[h]) for h in heads]

    gl = [x[ch - 1:ch, :] for x in gc]
    st = [st_scr[h] for h in heads]
    w_s = [_mm1(w[h], st[h]) for h in heads]
    q_s = [_mm1(q[h] * egc[h], st[h]) for h in heads]
    v_new = [u[h] - w_s[h] for h in heads]
    a_v = [_mm1(attn[h], v_new[h]) for h in heads]
    kd_t = [(k[h] * jnp.exp(gl[h] - gc[h])).T for h in heads]
    k_v = [_mm1(kd_t[h], v_new[h]) for h in heads]
    for h in heads:
        st_scr[h] = st[h] * jnp.exp(gl[h]) + k_v[h]
        zz = bz_ref[:, sl[h]]
        o_ref[:, sl[h]] = (_rms(q_s[h] + a_v[h], dng_ref[...]) * (zz * _sigmoid(zz))).astype(o_ref.dtype)

    @pl.when(c == pl.num_programs(1) - 1)
    def _():
        s_ref[0] = st_scr[...]


def _delta_prompt(z, conv_w, conv0, s0, a_log, dt_bias, dn_g, *, batch, seq, chunk):
    assert chunk == B_HEAD_DIM
    nc = seq // chunk
    smem = pl.BlockSpec(memory_space=pltpu.SMEM)
    zb = lambda col: pl.BlockSpec((chunk, B_WIDTH), lambda n, c: (n * nc + c, col // B_WIDTH))
    return pl.pallas_call(
        _delta_prompt_kernel,
        grid=(batch, nc),
        in_specs=[
            smem, smem,
            zb(Z_BQ), zb(Z_BK), zb(Z_BV), zb(Z_BZ),
            pl.BlockSpec((chunk, LANES), lambda n, c: (n * nc + c, Z_BA // LANES)),
            pl.BlockSpec(conv_w.shape, lambda n, c: (0, 0)),
            pl.BlockSpec((1, CONV_W - 1, 3 * B_WIDTH), lambda n, c: (n, 0, 0)),
            pl.BlockSpec((1, B_HEADS, B_HEAD_DIM, B_HEAD_DIM), lambda n, c: (n, 0, 0, 0)),
            pl.BlockSpec((1, B_HEAD_DIM), lambda n, c: (0, 0)),
        ],
        out_specs=[
            pl.BlockSpec((chunk, B_WIDTH), lambda n, c: (n * nc + c, 0)),
            pl.BlockSpec((1, B_HEADS, B_HEAD_DIM, B_HEAD_DIM), lambda n, c: (n, 0, 0, 0)),
        ],
        out_shape=[
            jax.ShapeDtypeStruct((batch * seq, B_WIDTH), BF16),
            jax.ShapeDtypeStruct((batch, B_HEADS, B_HEAD_DIM, B_HEAD_DIM), F32),
        ],
        scratch_shapes=[
            pltpu.VMEM((3, chunk + SUBLANES, B_WIDTH), F32),
            pltpu.VMEM((B_HEADS, B_HEAD_DIM, B_HEAD_DIM), F32),
        ],
        compiler_params=_cparams(("parallel", "arbitrary")),
        name="delta_prompt",
    )(a_log, dt_bias, z, z, z, z, z, conv_w, conv0, s0, dn_g)


def _delta_sample_kernel(alog_ref, dtb_ref, bq_ref, bk_ref, bv_ref, bz_ref, ba_ref, cw_ref, cb_ref, s0_ref,
                         dng_ref, o_ref, s_ref, *, new_rows):
    d = B_HEAD_DIM
    heads = range(B_HEADS)
    cb = cb_ref[0]

    def conv_rows(x_ref, i):
        x = x_ref[0]
        seg = slice(i * B_WIDTH, (i + 1) * B_WIDTH)
        ext = [cb[r:r + 1, seg] for r in range(CONV_W - 1)] + [x[t:t + 1, :] for t in range(new_rows)]
        out = []
        for t in range(new_rows):
            acc = ext[t] * cw_ref[0:1, seg]
            for j in range(1, CONV_W):
                acc = acc + ext[t + j] * cw_ref[j:j + 1, seg]
            out.append(acc * _sigmoid(acc))
        return out

    qs, ks, vs = conv_rows(bq_ref, 0), conv_rows(bk_ref, 1), conv_rows(bv_ref, 2)
    ba = ba_ref[0]
    z = bz_ref[0]
    sl = [slice(h * d, (h + 1) * d) for h in heads]
    beta = [_sigmoid(_lane_pick(ba, h)) for h in heads]
    g = [-jnp.exp(jnp.full((1, 1), alog_ref[h], F32)) * _softplus(_lane_pick(ba, B_HEADS + h) + dtb_ref[h])
         for h in heads]
    eye = lax.broadcasted_iota(jnp.int32, (d, d), 0) == lax.broadcasted_iota(jnp.int32, (d, d), 1)
    to_col = lambda r: jnp.sum(jnp.where(eye, r, 0.0), axis=-1, keepdims=True)
    unit = lambda r: r * lax.rsqrt(jnp.sum(r * r, axis=-1, keepdims=True) + EPS)

    st = [s0_ref[0, h] for h in heads]
    o_ref[0] = jnp.zeros(o_ref.shape[1:], F32)
    for t in range(new_rows):
        for h in heads:
            a = jnp.exp(g[h][t:t + 1, :])
            kc = to_col(unit(ks[t][:, sl[h]]))
            k_s = jnp.sum(kc * st[h], axis=0, keepdims=True)
            st[h] = a * st[h] + kc * (beta[h][t:t + 1, :] * (vs[t][:, sl[h]] - a * k_s))
            qc = to_col(unit(qs[t][:, sl[h]]) * (d ** -0.5))
            o = jnp.sum(qc * st[h], axis=0, keepdims=True)
            zz = z[t:t + 1, sl[h]]
            o_ref[0, t:t + 1, sl[h]] = _rms(o, dng_ref[...]) * (zz * _sigmoid(zz))
    for h in heads:
        s_ref[0, h] = st[h]


def _delta_sample(z3, conv_w, conv_buf, s0, a_log, dt_bias, dn_g, *, layer, new_rows):
    n, rows, _ = z3.shape
    smem = pl.BlockSpec(memory_space=pltpu.SMEM)
    zb = lambda col: pl.BlockSpec((1, rows, B_WIDTH), lambda b: (b, 0, col // B_WIDTH))
    state = (B_HEADS, B_HEAD_DIM, B_HEAD_DIM)
    kern = functools.partial(_delta_sample_kernel, new_rows=new_rows)
    return pl.pallas_call(
        kern,
        grid=(n,),
        in_specs=[
            smem, smem,
            zb(Z_BQ), zb(Z_BK), zb(Z_BV), zb(Z_BZ),
            pl.BlockSpec((1, rows, LANES), lambda b: (b, 0, Z_BA // LANES)),
            pl.BlockSpec(conv_w.shape, lambda b: (0, 0)),
            pl.BlockSpec((None, 1, CONV_W - 1, 3 * B_WIDTH), lambda b: (layer, b, 0, 0)),
            pl.BlockSpec((None, 1) + state, lambda b: (layer, b, 0, 0, 0)),
            pl.BlockSpec((1, B_HEAD_DIM), lambda b: (0, 0)),
        ],
        out_specs=[
            pl.BlockSpec((1, rows, B_WIDTH), lambda b: (b, 0, 0)),
            pl.BlockSpec((1,) + state, lambda b: (b, 0, 0, 0)),
        ],
        out_shape=[
            jax.ShapeDtypeStruct((n, rows, B_WIDTH), F32),
            jax.ShapeDtypeStruct((n,) + state, F32),
        ],
        compiler_params=_cparams(("arbitrary",)),
        name="delta_sample",
    )(a_log, dt_bias, z3, z3, z3, z3, z3, conv_w, conv_buf, s0, dn_g)


def _layout_w_in(w_in):
    order = np.asarray(C_HEAD_ORDER)
    cq = w_in[:, :, O_CQ:O_CKV].reshape(DEPTH, D_MODEL, C_HEADS, HEAD_DIM)[:, :, order]
    cq = cq.reshape(DEPTH, D_MODEL, C_WIDTH)
    pad = lambda n: jnp.zeros((DEPTH, D_MODEL, n), w_in.dtype)
    parts = [w_in[:, :, :O_BQKV], cq, w_in[:, :, O_BQKV:O_BETA], pad(Z_CK - Z_BZ - B_WIDTH),
             w_in[:, :, O_CKV:O_END], w_in[:, :, O_BETA:O_CQ], pad(Z_COLS - Z_BA - 2 * B_HEADS)]
    return jnp.concatenate(parts, axis=-1).astype(BF16)


def _layout_w_out(w_out):
    order = np.asarray(C_HEAD_ORDER)
    c_rows = w_out[:, A_WIDTH + B_WIDTH:].reshape(DEPTH, C_HEADS, HEAD_DIM, D_MODEL)[:, order]
    c_rows = c_rows.reshape(DEPTH, C_WIDTH, D_MODEL)
    return jnp.concatenate([w_out[:, :A_WIDTH + B_WIDTH], c_rows], axis=1).astype(BF16)


def _pair_table(v):
    return jnp.asarray(v, F32).reshape(-1, 2)


def _feature_major(cache):
    depth, n, tokens = cache.shape[:3]
    return jnp.transpose(cache, (0, 1, 3, 4, 5, 2)).reshape(depth, n, -1, tokens)


def kernel(x_prompt, x_sample, cache_dilated_kv, cache_swa_kv, state_delta_s, state_delta_conv, g_pre_mix,
           w_in, delta_conv_w, delta_a_log, delta_dt_bias, delta_norm_g, swa_sinks, w_out, g_post_mix,
           g_pre_mlp, w_up, w_down, g_post_mlp):
    batch, seq, _ = x_prompt.shape
    dec_batch, dec_seq, _ = x_sample.shape
    past_a = cache_dilated_kv.shape[2]
    past_c = cache_swa_kv.shape[2]
    assert seq % (TQ * A_CLASS_BLOCKS) == 0 and seq == A_WINDOW_MAX and dec_seq <= SAMPLE_ROWS
    assert past_a == A_WINDOW_MAX and past_c == C_WINDOW

    w_in_p = _layout_w_in(w_in)
    w_out_p = _layout_w_out(w_out)
    w_up_b = w_up.astype(BF16)
    w_dn_b = w_down.astype(BF16)
    cache_a = _feature_major(cache_dilated_kv)
    cache_c = _feature_major(cache_swa_kv)

    tab_pa = _prompt_tables_a(seq)
    tab_pc = _prompt_tables_c()
    tab_sa = _sample_tables(_log_mult_a, past_a, SAMPLE_ROWS)
    tab_sc = _sample_tables(_log_mult_c, past_c, SAMPLE_ROWS)
    alibi_a = _alibi(A_HEADS)
    alibi_c = _alibi(C_HEADS)[np.asarray(C_HEAD_ORDER)]
    slopes_a = _pair_table(alibi_a)
    slopes_c = _pair_table(alibi_c)
    no_sinks = jnp.zeros_like(slopes_a)
    c_order = np.asarray(C_HEAD_ORDER)

    conv0 = jnp.zeros((batch, CONV_W - 1, 3 * B_WIDTH), F32)
    s_zero = jnp.zeros((batch, B_HEADS, B_HEAD_DIM, B_HEAD_DIM), F32)

    xp = x_prompt.reshape(batch * seq, D_MODEL)
    xs = jnp.pad(x_sample, ((0, 0), (0, SAMPLE_ROWS - dec_seq), (0, 0))).reshape(dec_batch * SAMPLE_ROWS, D_MODEL)
    ms = xs.shape[0]

    p_s, p_conv = [], []
    s_akv, s_ckv, s_s, s_conv = [], [], [], []
    kv_t = None
    for l in range(DEPTH):
        row = lambda a: a[l].reshape(1, -1)
        sinks_c = _pair_table(swa_sinks[l][c_order])
        zp, kv_t = _inproj(xp, row(g_pre_mix), w_in_p, l, 512, kv_t=kv_t, seq=seq)
        zs, _ = _inproj(xs, row(g_pre_mix), w_in_p, l, ms)
        zs3 = zs.reshape(dec_batch, SAMPLE_ROWS, Z_COLS)

        a_p = _attn_a_prompt(zp, slopes_a, tab_pa, batch=batch, seq=seq)
        c_p = _attn_c_prompt(zp, slopes_c, sinks_c, tab_pc, batch=batch, seq=seq, blocks=4)
        b_p, st_p = _delta_prompt(zp, delta_conv_w[l], conv0, s_zero, delta_a_log[l], delta_dt_bias[l],
                                  row(delta_norm_g), batch=batch, seq=seq, chunk=128)

        a_s = _attn_sample(zs3, cache_a, alibi_a, no_sinks, tab_sa, layer=l, q_col=Z_AQ, k_col=Z_AK,
                           v_col=Z_AV, kv_width=A_WIDTH, kv_of_pair=lambda p: p, new_rows=dec_seq,
                           has_sink=False, name="attn_a_sample")
        c_s = _attn_sample(zs3, cache_c, alibi_c, sinks_c, tab_sc, layer=l, q_col=Z_CQ, k_col=Z_CK,
                           v_col=Z_CV, kv_width=C_KV_WIDTH, kv_of_pair=lambda p: p // 3, new_rows=dec_seq,
                           has_sink=True, name="attn_c_sample")
        b_s, st_s = _delta_sample(zs3, delta_conv_w[l], state_delta_conv, state_delta_s, delta_a_log[l],
                                  delta_dt_bias[l], row(delta_norm_g), layer=l, new_rows=dec_seq)

        mlp_w = (w_out_p, row(g_post_mix), row(g_pre_mlp), w_up_b, w_dn_b, row(g_post_mlp))
        xp = _mlp(a_p, b_p, c_p, xp, *mlp_w, l, 512, 512)
        xs = _mlp(a_s.reshape(ms, A_WIDTH), b_s.reshape(ms, B_WIDTH), c_s.reshape(ms, C_WIDTH), xs,
                  *mlp_w, l, ms, 512)

        zp3 = zp.reshape(batch, seq, Z_COLS)
        p_s.append(st_p)
        p_conv.append(zp3[:, seq - (CONV_W - 1):, Z_BQ:Z_BQ + 3 * B_WIDTH])
        s_akv.append(zs3[:, :dec_seq, Z_AK:Z_AK + 2 * A_WIDTH].reshape(dec_batch, dec_seq, 2, A_HEADS, HEAD_DIM))
        s_ckv.append(zs3[:, :dec_seq, Z_CK:Z_CK + 2 * C_KV_WIDTH]
                     .reshape(dec_batch, dec_seq, 2, C_KV_HEADS, HEAD_DIM))
        s_s.append(st_s)
        full = jnp.concatenate([state_delta_conv[l], zs3[:, :dec_seq, Z_BQ:Z_BQ + 3 * B_WIDTH]], axis=1)
        s_conv.append(full[:, full.shape[1] - (CONV_W - 1):])

    yp = xp.reshape(batch, seq, D_MODEL)
    ys = xs.reshape(dec_batch, SAMPLE_ROWS, D_MODEL)[:, :dec_seq]
    token_major = lambda t, heads: jnp.transpose(
        t.reshape(DEPTH, batch, 2, heads, HEAD_DIM, t.shape[-1]), (0, 1, 5, 2, 3, 4))
    p_akv = token_major(kv_t[0][..., seq - A_WINDOW_MAX:], A_HEADS)
    p_ckv = token_major(kv_t[1][..., seq - C_WINDOW:], C_KV_HEADS)
    return (yp, ys, p_akv, p_ckv, jnp.stack(p_s), jnp.stack(p_conv),
            jnp.stack(s_akv), jnp.stack(s_ckv), jnp.stack(s_s), jnp.stack(s_conv))
```

```python
import functools

import numpy as np
import jax
import jax.numpy as jnp
from jax import lax
from jax.experimental import pallas as pl
from jax.experimental.pallas import tpu as pltpu

F32 = jnp.float32
BF16 = jnp.bfloat16

D_MODEL = 2048
DEPTH = 4
HEAD_DIM = 64
A_HEADS = 12
A_BRANCHES = ((128, 1), (512, 4), (2048, 16))
A_WINDOW_MAX = 2048
B_HEAD_DIM = 128
B_HEADS = 4
CONV_W = 4
C_HEADS = 12
C_KV_HEADS = 4
C_WINDOW = 128
A_WIDTH = A_HEADS * HEAD_DIM
B_WIDTH = B_HEADS * B_HEAD_DIM
C_WIDTH = C_HEADS * HEAD_DIM
C_KV_WIDTH = C_KV_HEADS * HEAD_DIM
D_FF = 4 * D_MODEL
EPS = 1e-6
ATTN_SCALE = HEAD_DIM ** -0.5
LOG2E = 1.4426950408889634
NEG = -1e30

LANES = 128
SUBLANES = 8
VMEM_LIMIT = 56 * 1024 * 1024

Z_TILE = 768
Z_AQ, Z_AK, Z_AV = 0, 768, 1536
Z_CQ = 2304
Z_BQ, Z_BK, Z_BV, Z_BZ = 3072, 3584, 4096, 4608
Z_CK, Z_CV = 5376, 5632
Z_BA = 5888
Z_COLS = 6144
O_BQKV, O_BZ, O_BETA, O_CQ, O_CKV, O_END = 2304, 3840, 4352, 4360, 5128, 5640
C_HEAD_ORDER = (0, 3, 1, 4, 2, 5, 6, 9, 7, 10, 8, 11)
SAMPLE_ROWS = 8
TQ = 128
A_CLASS_BLOCKS = 4

_NN = (((1,), (0,)), ((), ()))
_NT = (((1,), (1,)), ((), ()))


def _alibi(n):
    return np.asarray([2.0 ** (-8.0 * (i + 1) / n) for i in range(n)], dtype=np.float32)


def _cparams(sem):
    return pltpu.CompilerParams(dimension_semantics=sem, vmem_limit_bytes=VMEM_LIMIT)


def _rms(x, g):
    return x * lax.rsqrt(jnp.mean(x * x, axis=-1, keepdims=True) + EPS) * g


def _sigmoid(x):
    return 1.0 / (1.0 + jnp.exp(-x))


def _inproj_kernel(x_ref, g_ref, w_ref, *refs, feature_major):
    j = pl.program_id(1)
    h_scr = refs[-1]
    z_ref = refs[-4] if feature_major else refs[-2]

    @pl.when(j == 0)
    def _():
        h_scr[...] = _rms(x_ref[...], g_ref[...]).astype(BF16)

    z_ref[...] = jnp.dot(h_scr[...], w_ref[...], preferred_element_type=F32)

    if feature_major:
        ta_ref, tc_ref = refs[-3], refs[-2]

        @pl.when((j == Z_AK // Z_TILE) | (j == Z_AV // Z_TILE))
        def _():
            ta_ref[...] = z_ref[...].T

        @pl.when(j == Z_CK // Z_TILE)
        def _():
            tc_ref[...] = z_ref[:, 0:2 * C_KV_WIDTH].T


def _inproj(x, g, w_all, layer, tm, kv_t=None, seq=None):
    m = x.shape[0]
    in_specs = [
        pl.BlockSpec((tm, D_MODEL), lambda i, j: (i, 0)),
        pl.BlockSpec((1, D_MODEL), lambda i, j: (0, 0)),
        pl.BlockSpec((None, D_MODEL, Z_TILE), lambda i, j: (layer, 0, j)),
    ]
    out_specs = [pl.BlockSpec((tm, Z_TILE), lambda i, j: (i, j))]
    out_shape = [jax.ShapeDtypeStruct((m, Z_COLS), F32)]
    args, aliases = [x, g, w_all], {}
    if seq is not None:
        tps = seq // tm
        batch = m // seq
        a_tile = lambda j: jnp.clip(j - Z_AK // Z_TILE, 0, 1)
        out_specs += [
            pl.BlockSpec((None, None, Z_TILE, tm), lambda i, j: (layer, i // tps, a_tile(j), i % tps)),
            pl.BlockSpec((None, None, 2 * C_KV_WIDTH, tm), lambda i, j: (layer, i // tps, 0, i % tps)),
        ]
        out_shape += [jax.ShapeDtypeStruct((DEPTH, batch, 2 * A_WIDTH, seq), F32),
                      jax.ShapeDtypeStruct((DEPTH, batch, 2 * C_KV_WIDTH, seq), F32)]
        if kv_t is not None:
            in_specs += [pl.BlockSpec(memory_space=pl.ANY)] * 2
            aliases = {3: 1, 4: 2}
            args += list(kv_t)
    out = pl.pallas_call(
        functools.partial(_inproj_kernel, feature_major=seq is not None),
        grid=(m // tm, Z_COLS // Z_TILE),
        in_specs=in_specs,
        out_specs=out_specs,
        out_shape=out_shape,
        input_output_aliases=aliases,
        scratch_shapes=[pltpu.VMEM((tm, D_MODEL), BF16)],
        compiler_params=_cparams(("parallel", "arbitrary")),
        name="inproj",
    )(*args)
    return out[0], tuple(out[1:])


def _mlp_kernel(ma_ref, mb_ref, mc_ref, x_ref, wo_ref, gpm_ref, gpre_ref, wup_ref, wdn_ref, gpost_ref,
                y_ref, hm_scr, acc_scr):
    f = pl.program_id(1)

    @pl.when(f == 0)
    def _():
        y = jnp.dot(ma_ref[...].astype(BF16), wo_ref[0:A_WIDTH, :], preferred_element_type=F32)
        y += jnp.dot(mb_ref[...].astype(BF16), wo_ref[A_WIDTH:A_WIDTH + B_WIDTH, :],
                     preferred_element_type=F32)
        y += jnp.dot(mc_ref[...].astype(BF16), wo_ref[A_WIDTH + B_WIDTH:, :], preferred_element_type=F32)
        x1 = x_ref[...] + _rms(y, gpm_ref[...])
        y_ref[...] = x1
        hm_scr[...] = _rms(x1, gpre_ref[...]).astype(BF16)
        acc_scr[...] = jnp.zeros_like(acc_scr)

    u = jnp.dot(hm_scr[...], wup_ref[...], preferred_element_type=F32)
    u = jnp.square(jnp.maximum(u, 0.0)).astype(BF16)
    acc_scr[...] += jnp.dot(u, wdn_ref[...], preferred_element_type=F32)

    @pl.when(f == pl.num_programs(1) - 1)
    def _():
        y_ref[...] += _rms(acc_scr[...], gpost_ref[...])


def _mlp(ma, mb, mc, x, wo_all, gpm, gpre, wup_all, wdn_all, gpost, layer, tm, tf):
    m = x.shape[0]
    row = lambda i, f: (i, 0)
    vec = pl.BlockSpec((1, D_MODEL), lambda i, f: (0, 0))
    return pl.pallas_call(
        _mlp_kernel,
        grid=(m // tm, D_FF // tf),
        in_specs=[
            pl.BlockSpec((tm, A_WIDTH), row),
            pl.BlockSpec((tm, B_WIDTH), row),
            pl.BlockSpec((tm, C_WIDTH), row),
            pl.BlockSpec((tm, D_MODEL), row),
            pl.BlockSpec((None, D_MODEL, D_MODEL), lambda i, f: (layer, 0, 0)),
            vec, vec,
            pl.BlockSpec((None, D_MODEL, tf), lambda i, f: (layer, 0, f)),
            pl.BlockSpec((None, tf, D_MODEL), lambda i, f: (layer, f, 0)),
            vec,
        ],
        out_specs=pl.BlockSpec((tm, D_MODEL), row),
        out_shape=jax.ShapeDtypeStruct((m, D_MODEL), F32),
        scratch_shapes=[pltpu.VMEM((tm, D_MODEL), BF16), pltpu.VMEM((tm, D_MODEL), F32)],
        compiler_params=_cparams(("parallel", "arbitrary")),
        name="outproj_mlp",
    )(ma, mb, mc, x, wo_all, gpm, gpre, wup_all, wdn_all, gpost)


def _log_mult_a(d):
    c = np.zeros(d.shape, np.int64)
    for window, dil in A_BRANCHES:
        c += ((d >= 0) & (d <= window) & (d % dil == 0)).astype(np.int64)
    return np.where(c > 0, np.log(np.maximum(c, 1)), NEG).astype(np.float32)


def _log_mult_c(d):
    return np.where((d >= 0) & (d <= C_WINDOW), 0.0, NEG).astype(np.float32)


def _tables(log_mult, d):
    return jnp.asarray(np.maximum(d, 0).astype(np.float32)), jnp.asarray(log_mult(d))


def _prompt_tables_a(seq):
    d = np.arange(TQ)[:, None] - np.arange(seq)[None, :] + (seq - TQ)
    return _tables(_log_mult_a, d)


def _prompt_tables_c():
    i = np.arange(TQ)[:, None]
    j = np.arange(2 * TQ)[None, :]
    return _tables(_log_mult_c, np.stack([TQ + i - j, i - j]))


def _sample_tables(log_mult, past, rows):
    d_c = past + np.arange(rows)[:, None] - np.arange(past)[None, :]
    d_n = np.arange(rows)[:, None] - np.arange(LANES)[None, :]
    d_n = np.where(np.arange(LANES)[None, :] < rows, d_n, -1)
    return _tables(log_mult, d_c) + _tables(log_mult, d_n)


def _head_masks(rows):
    left = lax.broadcasted_iota(jnp.int32, (rows, LANES), 1) < HEAD_DIM
    return left, (lambda x: jnp.where(left, x, 0.0), lambda x: jnp.where(left, 0.0, x))


def _stack_heads(q, pick):
    return jnp.concatenate([pick[0](q), pick[1](q)], axis=0)


def _softmax_pv(s, v, extra=None, base2=False):
    ex = jnp.exp2 if base2 else jnp.exp
    m = jnp.max(s, axis=-1, keepdims=True)
    pr = ex(s - m)
    l = jnp.sum(pr, axis=-1, keepdims=True)
    if extra is not None:
        l = l + ex(extra - m)
    return jnp.dot(pr.astype(BF16), v, preferred_element_type=F32) / l


def _attn_a_prompt_kernel(slope_ref, q_ref, k_ref, v_ref, d_ref, l_ref, o_ref, bias_scr, k_scr, v_scr, *,
                          n_cls):
    p = pl.program_id(1)
    c = pl.program_id(2)
    seq = k_ref.shape[0]
    left, pick = _head_masks(TQ)

    @pl.when(c == 0)
    def _():
        for h in range(2):
            bias_scr[h * TQ:(h + 1) * TQ, :] = (l_ref[...] - slope_ref[p, h] * d_ref[...]) * LOG2E
        k_scr[...] = k_ref[...].astype(BF16)
        v_scr[...] = v_ref[...].astype(BF16)

    for cc in range(n_cls):
        @pl.when(c == cc)
        def _(cc=cc):
            keys = [TQ * (cc * A_CLASS_BLOCKS + j + 1) for j in range(A_CLASS_BLOCKS)]

            def scores(j):
                qq = _stack_heads(q_ref[j * TQ:(j + 1) * TQ, :] * (ATTN_SCALE * LOG2E), pick).astype(BF16)
                s = lax.dot_general(qq, k_scr[0:keys[j], :], _NT, preferred_element_type=F32)
                return s + bias_scr[:, seq - keys[j]:seq]

            ahead = 2
            s = {j: scores(j) for j in range(min(ahead, A_CLASS_BLOCKS))}
            for j in range(A_CLASS_BLOCKS):
                o = _softmax_pv(s.pop(j), v_scr[0:keys[j], :], base2=True)
                o_ref[j * TQ:(j + 1) * TQ, :] = jnp.where(left, o[:TQ], o[TQ:]).astype(o_ref.dtype)
                if j + ahead < A_CLASS_BLOCKS:
                    s[j + ahead] = scores(j + ahead)


def _attn_a_prompt(z, slopes, tables, *, batch, seq):
    pairs = A_HEADS // 2
    rows = TQ * A_CLASS_BLOCKS
    n_cls = seq // rows
    dtab, ltab = tables
    const2 = lambda b, p, c: (0, 0)
    kern = functools.partial(_attn_a_prompt_kernel, n_cls=n_cls)
    return pl.pallas_call(
        kern,
        grid=(batch, pairs, n_cls),
        in_specs=[
            pl.BlockSpec(memory_space=pltpu.SMEM),
            pl.BlockSpec((rows, LANES), lambda b, p, c: (b * n_cls + c, Z_AQ // LANES + p)),
            pl.BlockSpec((seq, LANES), lambda b, p, c: (b, Z_AK // LANES + p)),
            pl.BlockSpec((seq, LANES), lambda b, p, c: (b, Z_AV // LANES + p)),
            pl.BlockSpec(dtab.shape, const2),
            pl.BlockSpec(ltab.shape, const2),
        ],
        out_specs=pl.BlockSpec((rows, LANES), lambda b, p, c: (b * n_cls + c, p)),
        out_shape=jax.ShapeDtypeStruct((batch * seq, A_WIDTH), BF16),
        scratch_shapes=[pltpu.VMEM((2 * TQ, seq), F32), pltpu.VMEM((seq, LANES), BF16),
                        pltpu.VMEM((seq, LANES), BF16)],
        compiler_params=_cparams(("parallel", "parallel", "arbitrary")),
        name="attn_a_prompt",
    )(slopes, z, z, z, dtab, ltab)


def _attn_c_prompt_kernel(slope_ref, sink_ref, q_ref, k_ref, v_ref, d_ref, l_ref, o_ref, *, blocks):
    p = pl.program_id(1)
    c = pl.program_id(2)
    base = c * (blocks * TQ)
    left, pick = _head_masks(TQ)
    first = c == 0
    bias = [jnp.concatenate([l_ref[t] - slope_ref[p, h] * d_ref[t] for h in range(2)], axis=0)
            for t in range(2)]
    upper = lax.broadcasted_iota(jnp.int32, (2 * TQ, 1), 0) < TQ
    sink = jnp.where(upper, sink_ref[p, 0], sink_ref[p, 1])
    work = []
    for j in range(blocks):
        start = jnp.maximum(base - TQ, 0) if j == 0 else base + (j - 1) * TQ
        start = pl.multiple_of(start, TQ)
        qq = _stack_heads(q_ref[j * TQ:(j + 1) * TQ, :] * ATTN_SCALE, pick).astype(BF16)
        k = k_ref[pl.ds(start, 2 * TQ), :].astype(BF16)
        b = jnp.where(first, bias[1], bias[0]) if j == 0 else bias[0]
        work.append((j, start, lax.dot_general(qq, k, _NT, preferred_element_type=F32) + b))
    for j, start, s in work:
        o = _softmax_pv(s, v_ref[pl.ds(start, 2 * TQ), :].astype(BF16), extra=sink)
        o_ref[j * TQ:(j + 1) * TQ, :] = jnp.where(left, o[:TQ], o[TQ:]).astype(o_ref.dtype)


def _attn_c_prompt(z, slopes, sinks, tables, *, batch, seq, blocks):
    pairs = C_HEADS // 2
    rows = TQ * blocks
    nchunk = seq // rows
    dtab, ltab = tables
    smem = pl.BlockSpec(memory_space=pltpu.SMEM)
    const3 = lambda b, p, c: (0, 0, 0)
    kern = functools.partial(_attn_c_prompt_kernel, blocks=blocks)
    return pl.pallas_call(
        kern,
        grid=(batch, pairs, nchunk),
        in_specs=[
            smem, smem,
            pl.BlockSpec((rows, LANES), lambda b, p, c: (b * nchunk + c, Z_CQ // LANES + p)),
            pl.BlockSpec((seq, LANES), lambda b, p, c: (b, Z_CK // LANES + p // 3)),
            pl.BlockSpec((seq, LANES), lambda b, p, c: (b, Z_CV // LANES + p // 3)),
            pl.BlockSpec(dtab.shape, const3),
            pl.BlockSpec(ltab.shape, const3),
        ],
        out_specs=pl.BlockSpec((rows, LANES), lambda b, p, c: (b * nchunk + c, p)),
        out_shape=jax.ShapeDtypeStruct((batch * seq, C_WIDTH), BF16),
        compiler_params=_cparams(("parallel", "parallel", "arbitrary")),
        name="attn_c_prompt",
    )(slopes, sinks, z, z, z, dtab, ltab)


def _attn_sample_kernel(sink_ref, q_ref, kn_ref, vn_ref, c_ref, dc_ref, lc_ref, dn_ref, ln_ref, o_ref, *,
                        slopes, kv_of_pair, new_rows, has_sink):
    rows = q_ref.shape[1]
    pairs = len(slopes) // 2
    v_row = c_ref.shape[1] // 2
    left, pick = _head_masks(rows)
    upper = lax.broadcasted_iota(jnp.int32, (2 * rows, 1), 0) < rows
    stacked_bias = lambda p, l_ref, d_ref: jnp.concatenate(
        [l_ref[...] - slopes[2 * p + h] * d_ref[...] for h in range(2)], axis=0)
    feat = lambda p: slice(kv_of_pair(p) * LANES, (kv_of_pair(p) + 1) * LANES)

    work = []
    for p in range(pairs):
        qq = _stack_heads(q_ref[0, :, p * LANES:(p + 1) * LANES] * ATTN_SCALE, pick)
        kc = c_ref[0, feat(p), :].astype(BF16)
        s_c = jnp.dot(qq.astype(BF16), kc, preferred_element_type=F32) + stacked_bias(p, lc_ref, dc_ref)
        work.append((qq, s_c))
    for p, (qq, s_c) in enumerate(work):
        kn = kn_ref[0, :, feat(p)]
        vn = vn_ref[0, :, feat(p)]
        bias_n = stacked_bias(p, ln_ref, dn_ref)
        s_n = [jnp.sum(qq * kn[j:j + 1, :], axis=-1, keepdims=True) + bias_n[:, j:j + 1]
               for j in range(new_rows)]
        m = jnp.max(s_c, axis=-1, keepdims=True)
        for s in s_n:
            m = jnp.maximum(m, s)
        p_c = jnp.exp(s_c - m)
        l = jnp.sum(p_c, axis=-1, keepdims=True)
        vc = c_ref[0, v_row + kv_of_pair(p) * LANES:v_row + (kv_of_pair(p) + 1) * LANES, :].astype(BF16)
        o = lax.dot_general(p_c.astype(BF16), vc, _NT, preferred_element_type=F32)
        for j, s in enumerate(s_n):
            p_n = jnp.exp(s - m)
            l = l + p_n
            o = o + p_n * vn[j:j + 1, :]
        if has_sink:
            l = l + jnp.exp(jnp.where(upper, sink_ref[p, 0], sink_ref[p, 1]) - m)
        o = o / l
        o_ref[0, :, p * LANES:(p + 1) * LANES] = jnp.where(left, o[:rows], o[rows:])


def _attn_sample(z3, cache_t, slopes, sinks, tables, *, layer, q_col, k_col, v_col, kv_width,
                 kv_of_pair, new_rows, has_sink, name):
    n, rows, _ = z3.shape
    feats, past = cache_t.shape[2:]
    width = len(slopes) // 2 * LANES
    const2 = lambda b: (0, 0)
    zblk = lambda col, w: pl.BlockSpec((1, rows, w), lambda b: (b, 0, col // w))
    kern = functools.partial(_attn_sample_kernel, slopes=tuple(float(s) for s in slopes),
                             kv_of_pair=kv_of_pair, new_rows=new_rows, has_sink=has_sink)
    return pl.pallas_call(
        kern,
        grid=(n,),
        in_specs=[
            pl.BlockSpec(memory_space=pltpu.SMEM),
            zblk(q_col, width), zblk(k_col, kv_width), zblk(v_col, kv_width),
            pl.BlockSpec((None, 1, feats, past), lambda b: (layer, b, 0, 0)),
        ] + [pl.BlockSpec(t.shape, const2) for t in tables],
        out_specs=pl.BlockSpec((1, rows, width), lambda b: (b, 0, 0)),
        out_shape=jax.ShapeDtypeStruct((n, rows, width), F32),
        compiler_params=_cparams(("arbitrary",)),
        name=name,
    )(sinks, z3, z3, z3, cache_t, *tables)


def _bf16_pieces(a):
    p1 = a.astype(BF16)
    r1 = a - p1.astype(F32)
    p2 = r1.astype(BF16)
    return p1, p2, (r1 - p2.astype(F32)).astype(BF16)


def _mm1(a, b, dims=_NN):
    return lax.dot_general(a.astype(BF16), b.astype(BF16), dims, preferred_element_type=F32)


def _unit_lower_inverse(mats, row, col):
    blk = lambda s: (row >> s) == (col >> s)
    eye = jnp.where(row == col, 1.0, 0.0)
    pw = [jnp.where(blk(4), -a, 0.0) for a in mats]
    t = [eye + x for x in pw]
    for _ in range(3):
        pw = [_mm1(x, x).astype(BF16) for x in pw]
        t = [x + _mm1(x, y) for x, y in zip(t, pw)]
    for s in (4, 5, 6):
        off = [jnp.where(blk(s + 1), jnp.where(blk(s), 0.0, a), 0.0).astype(BF16) for a in mats]
        tb = [x.astype(BF16) for x in t]
        mid = [_mm1(o, y) for o, y in zip(off, tb)]
        t = [x - _mm1(y, m) for x, y, m in zip(t, tb, mid)]
    return t


def _softplus(x):
    return jnp.maximum(x, 0.0) + jnp.log(1.0 + jnp.exp(-jnp.abs(x)))


def _lane_pick(x, idx):
    lane = lax.broadcasted_iota(jnp.int32, x.shape, 1)
    return jnp.sum(jnp.where(lane == idx, x, 0.0), axis=-1, keepdims=True)


def _delta_prompt_kernel(alog_ref, dtb_ref, bq_ref, bk_ref, bv_ref, bz_ref, ba_ref, cw_ref, cb_ref, s0_ref,
                         dng_ref, o_ref, s_ref, ext_scr, st_scr):
    c = pl.program_id(1)
    group, ch = bq_ref.shape[:2]
    heads = range(B_HEADS)
    chains = [(n, h) for n in range(group) for h in heads]
    every = range(len(chains))

    @pl.when(c == 0)
    def _():
        ext_scr[:, :, 0:SUBLANES, :] = jnp.zeros((group, 3, SUBLANES, B_WIDTH), F32)
        for n in range(group):
            for i in range(3):
                ext_scr[n, i, SUBLANES - (CONV_W - 1):SUBLANES, :] = cb_ref[n, :, i * B_WIDTH:(i + 1) * B_WIDTH]
        st_scr[...] = s0_ref[...]

    conv = []
    for n in range(group):
        per_seq = []
        for i, ref in enumerate((bq_ref, bk_ref, bv_ref)):
            ext_scr[n, i, SUBLANES:SUBLANES + ch, :] = ref[n]
            acc = jnp.zeros((ch, B_WIDTH), F32)
            for j in range(CONV_W):
                w = cw_ref[j:j + 1, i * B_WIDTH:(i + 1) * B_WIDTH]
                acc = acc + ext_scr[n, i, pl.ds(SUBLANES - (CONV_W - 1) + j, ch), :] * w
            ext_scr[n, i, 0:SUBLANES, :] = ext_scr[n, i, ch:ch + SUBLANES, :]
            per_seq.append(acc * _sigmoid(acc))
        conv.append(per_seq)

    row = lax.broadcasted_iota(jnp.int32, (ch, ch), 0)
    col = lax.broadcasted_iota(jnp.int32, (ch, ch), 1)
    tri = jnp.where(row >= col, 1.0, 0.0).astype(BF16)
    dg = lambda x, y: lax.dot_general(x, y, _NN, preferred_element_type=F32)

    sl = [slice(h * B_HEAD_DIM, (h + 1) * B_HEAD_DIM) for h in heads]
    unit = lambda x: x * lax.rsqrt(jnp.sum(x * x, axis=-1, keepdims=True) + EPS)
    q = [unit(conv[n][0][:, sl[h]]) * (B_HEAD_DIM ** -0.5) for n, h in chains]
    k = [unit(conv[n][1][:, sl[h]]) for n, h in chains]
    v = [conv[n][2][:, sl[h]] for n, h in chains]
    pick = jnp.concatenate([jnp.where(row == c, 1.0, 0.0).astype(BF16) for c in range(2 * B_HEADS)], axis=1)
    spread = [sum(dg(piece, pick) for piece in _bf16_pieces(ba_ref[n])) for n in range(group)]
    beta = [_sigmoid(spread[n][:, sl[h]]) for n, h in chains]

    gc = []
    for n, h in chains:
        logit = spread[n][:, (B_HEADS + h) * LANES:(B_HEADS + h + 1) * LANES]
        g = -jnp.exp(jnp.full((1, 1), alog_ref[h], F32)) * _softplus(logit + dtb_ref[h])
        gc.append(sum(dg(tri, piece) for piece in _bf16_pieces(g)))

    decay = [jnp.exp(jnp.where(row >= col, x - x.T, NEG)) for x in gc]
    egc = [jnp.exp(x) for x in gc]
    kb = [k[i] * beta[i] for i in every]
    vb = [v[i] * beta[i] for i in every]
    akk = [_mm1(kb[i], k[i], _NT) for i in every]
    qk = [_mm1(q[i], k[i], _NT) for i in every]
    a = [jnp.where(row > col, akk[i] * decay[i], 0.0) for i in every]
    attn = [qk[i] * decay[i] for i in every]
    t = _unit_lower_inverse(a, row, col)
    u = [_mm1(t[i], vb[i]) for i in every]
    w = [_mm1(t[i], kb[i] * egc[i]) for i in every]

    gl = [x[ch - 1:ch, :] for x in gc]
    st = [st_scr[n, h] for n, h in chains]
    w_s = [_mm1(w[i], st[i]) for i in every]
    q_s = [_mm1(q[i] * egc[i], st[i]) for i in every]
    v_new = [u[i] - w_s[i] for i in every]
    a_v = [_mm1(attn[i], v_new[i]) for i in every]
    kd_t = [(k[i] * jnp.exp(gl[i] - gc[i])).T for i in every]
    k_v = [_mm1(kd_t[i], v_new[i]) for i in every]
    for i, (n, h) in enumerate(chains):
        st_scr[n, h] = st[i] * jnp.exp(gl[i]) + k_v[i]
        zz = bz_ref[n, :, sl[h]]
        o_ref[n, :, sl[h]] = (_rms(q_s[i] + a_v[i], dng_ref[...]) * (zz * _sigmoid(zz))).astype(o_ref.dtype)

    @pl.when(c == pl.num_programs(1) - 1)
    def _():
        s_ref[...] = st_scr[...]


def _delta_prompt(z3, conv_w, conv0, s0, a_log, dt_bias, dn_g, *, group, chunk):
    assert chunk == B_HEAD_DIM
    batch, seq, _ = z3.shape
    nc = seq // chunk
    smem = pl.BlockSpec(memory_space=pltpu.SMEM)
    state = (B_HEADS, B_HEAD_DIM, B_HEAD_DIM)
    zb = lambda col, w: pl.BlockSpec((group, chunk, w), lambda n, c: (n, c, col // w))
    return pl.pallas_call(
        _delta_prompt_kernel,
        grid=(batch // group, nc),
        in_specs=[
            smem, smem,
            zb(Z_BQ, B_WIDTH), zb(Z_BK, B_WIDTH), zb(Z_BV, B_WIDTH), zb(Z_BZ, B_WIDTH), zb(Z_BA, LANES),
            pl.BlockSpec(conv_w.shape, lambda n, c: (0, 0)),
            pl.BlockSpec((group, CONV_W - 1, 3 * B_WIDTH), lambda n, c: (n, 0, 0)),
            pl.BlockSpec((group,) + state, lambda n, c: (n, 0, 0, 0)),
            pl.BlockSpec((1, B_HEAD_DIM), lambda n, c: (0, 0)),
        ],
        out_specs=[
            pl.BlockSpec((group, chunk, B_WIDTH), lambda n, c: (n, c, 0)),
            pl.BlockSpec((group,) + state, lambda n, c: (n, 0, 0, 0)),
        ],
        out_shape=[
            jax.ShapeDtypeStruct((batch, seq, B_WIDTH), BF16),
            jax.ShapeDtypeStruct((batch,) + state, F32),
        ],
        scratch_shapes=[
            pltpu.VMEM((group, 3, chunk + SUBLANES, B_WIDTH), F32),
            pltpu.VMEM((group,) + state, F32),
        ],
        compiler_params=_cparams(("parallel", "arbitrary")),
        name="delta_prompt",
    )(a_log, dt_bias, z3, z3, z3, z3, z3, conv_w, conv0, s0, dn_g)


def _delta_sample_kernel(alog_ref, dtb_ref, bq_ref, bk_ref, bv_ref, bz_ref, ba_ref, cw_ref, cb_ref, s0_ref,
                         dng_ref, o_ref, s_ref, *, new_rows):
    d = B_HEAD_DIM
    heads = range(B_HEADS)
    cb = cb_ref[0]

    def conv_rows(x_ref, i):
        x = x_ref[0]
        seg = slice(i * B_WIDTH, (i + 1) * B_WIDTH)
        ext = [cb[r:r + 1, seg] for r in range(CONV_W - 1)] + [x[t:t + 1, :] for t in range(new_rows)]
        out = []
        for t in range(new_rows):
            acc = ext[t] * cw_ref[0:1, seg]
            for j in range(1, CONV_W):
                acc = acc + ext[t + j] * cw_ref[j:j + 1, seg]
            out.append(acc * _sigmoid(acc))
        return out

    qs, ks, vs = conv_rows(bq_ref, 0), conv_rows(bk_ref, 1), conv_rows(bv_ref, 2)
    ba = ba_ref[0]
    z = bz_ref[0]
    sl = [slice(h * d, (h + 1) * d) for h in heads]
    beta = [_sigmoid(_lane_pick(ba, h)) for h in heads]
    g = [-jnp.exp(jnp.full((1, 1), alog_ref[h], F32)) * _softplus(_lane_pick(ba, B_HEADS + h) + dtb_ref[h])
         for h in heads]
    eye = lax.broadcasted_iota(jnp.int32, (d, d), 0) == lax.broadcasted_iota(jnp.int32, (d, d), 1)
    to_col = lambda r: jnp.sum(jnp.where(eye, r, 0.0), axis=-1, keepdims=True)
    unit = lambda r: r * lax.rsqrt(jnp.sum(r * r, axis=-1, keepdims=True) + EPS)

    st = [s0_ref[0, h] for h in heads]
    o_ref[0] = jnp.zeros(o_ref.shape[1:], F32)
    for t in range(new_rows):
        for h in heads:
            a = jnp.exp(g[h][t:t + 1, :])
            kc = to_col(unit(ks[t][:, sl[h]]))
            k_s = jnp.sum(kc * st[h], axis=0, keepdims=True)
            st[h] = a * st[h] + kc * (beta[h][t:t + 1, :] * (vs[t][:, sl[h]] - a * k_s))
            qc = to_col(unit(qs[t][:, sl[h]]) * (d ** -0.5))
            o = jnp.sum(qc * st[h], axis=0, keepdims=True)
            zz = z[t:t + 1, sl[h]]
            o_ref[0, t:t + 1, sl[h]] = _rms(o, dng_ref[...]) * (zz * _sigmoid(zz))
    for h in heads:
        s_ref[0, h] = st[h]


def _delta_sample(z3, conv_w, conv_buf, s0, a_log, dt_bias, dn_g, *, layer, new_rows):
    n, rows, _ = z3.shape
    smem = pl.BlockSpec(memory_space=pltpu.SMEM)
    zb = lambda col: pl.BlockSpec((1, rows, B_WIDTH), lambda b: (b, 0, col // B_WIDTH))
    state = (B_HEADS, B_HEAD_DIM, B_HEAD_DIM)
    kern = functools.partial(_delta_sample_kernel, new_rows=new_rows)
    return pl.pallas_call(
        kern,
        grid=(n,),
        in_specs=[
            smem, smem,
            zb(Z_BQ), zb(Z_BK), zb(Z_BV), zb(Z_BZ),
            pl.BlockSpec((1, rows, LANES), lambda b: (b, 0, Z_BA // LANES)),
            pl.BlockSpec(conv_w.shape, lambda b: (0, 0)),
            pl.BlockSpec((None, 1, CONV_W - 1, 3 * B_WIDTH), lambda b: (layer, b, 0, 0)),
            pl.BlockSpec((None, 1) + state, lambda b: (layer, b, 0, 0, 0)),
            pl.BlockSpec((1, B_HEAD_DIM), lambda b: (0, 0)),
        ],
        out_specs=[
            pl.BlockSpec((1, rows, B_WIDTH), lambda b: (b, 0, 0)),
            pl.BlockSpec((1,) + state, lambda b: (b, 0, 0, 0)),
        ],
        out_shape=[
            jax.ShapeDtypeStruct((n, rows, B_WIDTH), F32),
            jax.ShapeDtypeStruct((n,) + state, F32),
        ],
        compiler_params=_cparams(("arbitrary",)),
        name="delta_sample",
    )(a_log, dt_bias, z3, z3, z3, z3, z3, conv_w, conv_buf, s0, dn_g)


def _layout_w_in(w_in):
    order = np.asarray(C_HEAD_ORDER)
    cq = w_in[:, :, O_CQ:O_CKV].reshape(DEPTH, D_MODEL, C_HEADS, HEAD_DIM)[:, :, order]
    cq = cq.reshape(DEPTH, D_MODEL, C_WIDTH)
    pad = lambda n: jnp.zeros((DEPTH, D_MODEL, n), w_in.dtype)
    parts = [w_in[:, :, :O_BQKV], cq, w_in[:, :, O_BQKV:O_BETA], pad(Z_CK - Z_BZ - B_WIDTH),
             w_in[:, :, O_CKV:O_END], w_in[:, :, O_BETA:O_CQ], pad(Z_COLS - Z_BA - 2 * B_HEADS)]
    return jnp.concatenate(parts, axis=-1).astype(BF16)


def _layout_w_out(w_out):
    order = np.asarray(C_HEAD_ORDER)
    c_rows = w_out[:, A_WIDTH + B_WIDTH:].reshape(DEPTH, C_HEADS, HEAD_DIM, D_MODEL)[:, order]
    c_rows = c_rows.reshape(DEPTH, C_WIDTH, D_MODEL)
    return jnp.concatenate([w_out[:, :A_WIDTH + B_WIDTH], c_rows], axis=1).astype(BF16)


def _pair_table(v):
    return jnp.asarray(v, F32).reshape(-1, 2)


def _feature_major(cache):
    depth, n, tokens = cache.shape[:3]
    return jnp.transpose(cache, (0, 1, 3, 4, 5, 2)).reshape(depth, n, -1, tokens)


def kernel(x_prompt, x_sample, cache_dilated_kv, cache_swa_kv, state_delta_s, state_delta_conv, g_pre_mix,
           w_in, delta_conv_w, delta_a_log, delta_dt_bias, delta_norm_g, swa_sinks, w_out, g_post_mix,
           g_pre_mlp, w_up, w_down, g_post_mlp):
    batch, seq, _ = x_prompt.shape
    dec_batch, dec_seq, _ = x_sample.shape
    past_a = cache_dilated_kv.shape[2]
    past_c = cache_swa_kv.shape[2]
    assert seq % (TQ * A_CLASS_BLOCKS) == 0 and seq == A_WINDOW_MAX and dec_seq <= SAMPLE_ROWS
    assert past_a == A_WINDOW_MAX and past_c == C_WINDOW

    w_in_p = _layout_w_in(w_in)
    w_out_p = _layout_w_out(w_out)
    w_up_b = w_up.astype(BF16)
    w_dn_b = w_down.astype(BF16)
    cache_a = _feature_major(cache_dilated_kv)
    cache_c = _feature_major(cache_swa_kv)

    tab_pa = _prompt_tables_a(seq)
    tab_pc = _prompt_tables_c()
    tab_sa = _sample_tables(_log_mult_a, past_a, SAMPLE_ROWS)
    tab_sc = _sample_tables(_log_mult_c, past_c, SAMPLE_ROWS)
    alibi_a = _alibi(A_HEADS)
    alibi_c = _alibi(C_HEADS)[np.asarray(C_HEAD_ORDER)]
    slopes_a = _pair_table(alibi_a)
    slopes_c = _pair_table(alibi_c)
    no_sinks = jnp.zeros_like(slopes_a)
    c_order = np.asarray(C_HEAD_ORDER)

    conv0 = jnp.zeros((batch, CONV_W - 1, 3 * B_WIDTH), F32)
    s_zero = jnp.zeros((batch, B_HEADS, B_HEAD_DIM, B_HEAD_DIM), F32)

    xp = x_prompt.reshape(batch * seq, D_MODEL)
    xs = jnp.pad(x_sample, ((0, 0), (0, SAMPLE_ROWS - dec_seq), (0, 0))).reshape(dec_batch * SAMPLE_ROWS, D_MODEL)
    ms = xs.shape[0]

    p_s, p_conv = [], []
    s_akv, s_ckv, s_s, s_conv = [], [], [], []
    kv_t = None
    for l in range(DEPTH):
        row = lambda a: a[l].reshape(1, -1)
        sinks_c = _pair_table(swa_sinks[l][c_order])
        zp, kv_t = _inproj(xp, row(g_pre_mix), w_in_p, l, 1024, kv_t=kv_t, seq=seq)
        zs, _ = _inproj(xs, row(g_pre_mix), w_in_p, l, ms)
        zs3 = zs.reshape(dec_batch, SAMPLE_ROWS, Z_COLS)

        a_p = _attn_a_prompt(zp, slopes_a, tab_pa, batch=batch, seq=seq)
        c_p = _attn_c_prompt(zp, slopes_c, sinks_c, tab_pc, batch=batch, seq=seq, blocks=8)
        zp3 = zp.reshape(batch, seq, Z_COLS)
        b_p, st_p = _delta_prompt(zp3, delta_conv_w[l], conv0, s_zero, delta_a_log[l], delta_dt_bias[l],
                                  row(delta_norm_g), group=batch, chunk=B_HEAD_DIM)
        b_p = b_p.reshape(batch * seq, B_WIDTH)

        a_s = _attn_sample(zs3, cache_a, alibi_a, no_sinks, tab_sa, layer=l, q_col=Z_AQ, k_col=Z_AK,
                           v_col=Z_AV, kv_width=A_WIDTH, kv_of_pair=lambda p: p, new_rows=dec_seq,
                           has_sink=False, name="attn_a_sample")
        c_s = _attn_sample(zs3, cache_c, alibi_c, sinks_c, tab_sc, layer=l, q_col=Z_CQ, k_col=Z_CK,
                           v_col=Z_CV, kv_width=C_KV_WIDTH, kv_of_pair=lambda p: p // 3, new_rows=dec_seq,
                           has_sink=True, name="attn_c_sample")
        b_s, st_s = _delta_sample(zs3, delta_conv_w[l], state_delta_conv, state_delta_s, delta_a_log[l],
                                  delta_dt_bias[l], row(delta_norm_g), layer=l, new_rows=dec_seq)

        mlp_w = (w_out_p, row(g_post_mix), row(g_pre_mlp), w_up_b, w_dn_b, row(g_post_mlp))
        xp = _mlp(a_p, b_p, c_p, xp, *mlp_w, l, 512, 512)
        xs = _mlp(a_s.reshape(ms, A_WIDTH), b_s.reshape(ms, B_WIDTH), c_s.reshape(ms, C_WIDTH), xs,
                  *mlp_w, l, ms, 512)

        p_s.append(st_p)
        p_conv.append(zp3[:, seq - (CONV_W - 1):, Z_BQ:Z_BQ + 3 * B_WIDTH])
        s_akv.append(zs3[:, :dec_seq, Z_AK:Z_AK + 2 * A_WIDTH].reshape(dec_batch, dec_seq, 2, A_HEADS, HEAD_DIM))
        s_ckv.append(zs3[:, :dec_seq, Z_CK:Z_CK + 2 * C_KV_WIDTH]
                     .reshape(dec_batch, dec_seq, 2, C_KV_HEADS, HEAD_DIM))
        s_s.append(st_s)
        full = jnp.concatenate([state_delta_conv[l], zs3[:, :dec_seq, Z_BQ:Z_BQ + 3 * B_WIDTH]], axis=1)
        s_conv.append(full[:, full.shape[1] - (CONV_W - 1):])

    yp = xp.reshape(batch, seq, D_MODEL)
    ys = xs.reshape(dec_batch, SAMPLE_ROWS, D_MODEL)[:, :dec_seq]
    token_major = lambda t, heads: jnp.transpose(
        t.reshape(DEPTH, batch, 2, heads, HEAD_DIM, t.shape[-1]), (0, 1, 5, 2, 3, 4))
    p_akv = token_major(kv_t[0][..., seq - A_WINDOW_MAX:], A_HEADS)
    p_ckv = token_major(kv_t[1][..., seq - C_WINDOW:], C_KV_HEADS)
    return (yp, ys, p_akv, p_ckv, jnp.stack(p_s), jnp.stack(p_conv),
            jnp.stack(s_akv), jnp.stack(s_ckv), jnp.stack(s_s), jnp.stack(s_conv))
```

```python
import functools

import numpy as np
import jax
import jax.numpy as jnp
from jax import lax
from jax.experimental import pallas as pl
from jax.experimental.pallas import tpu as pltpu

F32 = jnp.float32
BF16 = jnp.bfloat16

D_MODEL = 2048
DEPTH = 4
HEAD_DIM = 64
A_HEADS = 12
A_BRANCHES = ((128, 1), (512, 4), (2048, 16))
A_WINDOW_MAX = 2048
B_HEAD_DIM = 128
B_HEADS = 4
CONV_W = 4
C_HEADS = 12
C_KV_HEADS = 4
C_WINDOW = 128
A_WIDTH = A_HEADS * HEAD_DIM
B_WIDTH = B_HEADS * B_HEAD_DIM
C_WIDTH = C_HEADS * HEAD_DIM
C_KV_WIDTH = C_KV_HEADS * HEAD_DIM
D_FF = 4 * D_MODEL
EPS = 1e-6
ATTN_SCALE = HEAD_DIM ** -0.5
LOG2E = 1.4426950408889634
NEG = -1e30

LANES = 128
SUBLANES = 8
VMEM_LIMIT = 56 * 1024 * 1024

Z_TILE = 768
Z_AQ, Z_AK, Z_AV = 0, 768, 1536
Z_CQ = 2304
Z_BQ, Z_BK, Z_BV, Z_BZ = 3072, 3584, 4096, 4608
Z_CK, Z_CV = 5376, 5632
Z_BA = 5888
Z_COLS = 6144
O_BQKV, O_BZ, O_BETA, O_CQ, O_CKV, O_END = 2304, 3840, 4352, 4360, 5128, 5640
C_HEAD_ORDER = (0, 3, 1, 4, 2, 5, 6, 9, 7, 10, 8, 11)
SAMPLE_ROWS = 8
TQ = 128
A_CLASS_BLOCKS = 4

_NN = (((1,), (0,)), ((), ()))
_NT = (((1,), (1,)), ((), ()))


def _alibi(n):
    return np.asarray([2.0 ** (-8.0 * (i + 1) / n) for i in range(n)], dtype=np.float32)


def _cparams(sem):
    return pltpu.CompilerParams(dimension_semantics=sem, vmem_limit_bytes=VMEM_LIMIT)


def _rms(x, g):
    return x * lax.rsqrt(jnp.mean(x * x, axis=-1, keepdims=True) + EPS) * g


def _sigmoid(x):
    return 1.0 / (1.0 + jnp.exp(-x))


def _inproj_kernel(x_ref, g_ref, w_ref, *refs, feature_major):
    j = pl.program_id(1)
    h_scr = refs[-1]
    z_ref = refs[-4] if feature_major else refs[-2]

    @pl.when(j == 0)
    def _():
        h_scr[...] = _rms(x_ref[...], g_ref[...]).astype(BF16)

    z_ref[...] = lax.dot_general(h_scr[...], w_ref[...], _NT, preferred_element_type=F32)

    if feature_major:
        ta_ref, tc_ref = refs[-3], refs[-2]

        @pl.when((j == Z_AK // Z_TILE) | (j == Z_AV // Z_TILE))
        def _():
            ta_ref[...] = z_ref[...].T

        @pl.when(j == Z_CK // Z_TILE)
        def _():
            tc_ref[...] = z_ref[:, 0:2 * C_KV_WIDTH].T


def _inproj(x, g, w_all, layer, tm, kv_t=None, seq=None):
    m = x.shape[0]
    in_specs = [
        pl.BlockSpec((tm, D_MODEL), lambda i, j: (i, 0)),
        pl.BlockSpec((1, D_MODEL), lambda i, j: (0, 0)),
        pl.BlockSpec((None, Z_TILE, D_MODEL), lambda i, j: (layer, j, 0)),
    ]
    out_specs = [pl.BlockSpec((tm, Z_TILE), lambda i, j: (i, j))]
    out_shape = [jax.ShapeDtypeStruct((m, Z_COLS), F32)]
    args, aliases = [x, g, w_all], {}
    if seq is not None:
        tps = seq // tm
        a_tile = lambda j: jnp.clip(j - Z_AK // Z_TILE, 0, 1)
        out_specs += [
            pl.BlockSpec((None, None, Z_TILE, tm), lambda i, j: (layer, i // tps, a_tile(j), i % tps)),
            pl.BlockSpec((None, None, 2 * C_KV_WIDTH, tm), lambda i, j: (layer, i // tps, 0, i % tps)),
        ]
        out_shape += [jax.ShapeDtypeStruct(t.shape, t.dtype) for t in kv_t]
        in_specs += [pl.BlockSpec(memory_space=pl.ANY)] * 2
        aliases = {3: 1, 4: 2}
        args += list(kv_t)
    out = pl.pallas_call(
        functools.partial(_inproj_kernel, feature_major=seq is not None),
        grid=(m // tm, Z_COLS // Z_TILE),
        in_specs=in_specs,
        out_specs=out_specs,
        out_shape=out_shape,
        input_output_aliases=aliases,
        scratch_shapes=[pltpu.VMEM((tm, D_MODEL), BF16)],
        compiler_params=_cparams(("parallel", "arbitrary")),
        name="inproj",
    )(*args)
    return out[0], tuple(out[1:])


def _mlp_kernel(ma_ref, mb_ref, mc_ref, x_ref, wo_ref, gpm_ref, gpre_ref, wup_ref, wdn_ref, gpost_ref,
                y_ref, hm_scr, acc_scr):
    f = pl.program_id(1)

    @pl.when(f == 0)
    def _():
        y = jnp.dot(ma_ref[...].astype(BF16), wo_ref[0:A_WIDTH, :], preferred_element_type=F32)
        y += jnp.dot(mb_ref[...].astype(BF16), wo_ref[A_WIDTH:A_WIDTH + B_WIDTH, :],
                     preferred_element_type=F32)
        y += jnp.dot(mc_ref[...].astype(BF16), wo_ref[A_WIDTH + B_WIDTH:, :], preferred_element_type=F32)
        x1 = x_ref[...] + _rms(y, gpm_ref[...])
        y_ref[...] = x1
        hm_scr[...] = _rms(x1, gpre_ref[...]).astype(BF16)
        acc_scr[...] = jnp.zeros_like(acc_scr)

    u = jnp.dot(hm_scr[...], wup_ref[...], preferred_element_type=F32)
    u = jnp.square(jnp.maximum(u, 0.0)).astype(BF16)
    acc_scr[...] += jnp.dot(u, wdn_ref[...], preferred_element_type=F32)

    @pl.when(f == pl.num_programs(1) - 1)
    def _():
        y_ref[...] += _rms(acc_scr[...], gpost_ref[...])


def _mlp(ma, mb, mc, x, wo_all, gpm, gpre, wup_all, wdn_all, gpost, layer, tm, tf):
    m = x.shape[0]
    row = lambda i, f: (i, 0)
    vec = pl.BlockSpec((1, D_MODEL), lambda i, f: (0, 0))
    return pl.pallas_call(
        _mlp_kernel,
        grid=(m // tm, D_FF // tf),
        in_specs=[
            pl.BlockSpec((tm, A_WIDTH), row),
            pl.BlockSpec((tm, B_WIDTH), row),
            pl.BlockSpec((tm, C_WIDTH), row),
            pl.BlockSpec((tm, D_MODEL), row),
            pl.BlockSpec((None, D_MODEL, D_MODEL), lambda i, f: (layer, 0, 0)),
            vec, vec,
            pl.BlockSpec((None, D_MODEL, tf), lambda i, f: (layer, 0, f)),
            pl.BlockSpec((None, tf, D_MODEL), lambda i, f: (layer, f, 0)),
            vec,
        ],
        out_specs=pl.BlockSpec((tm, D_MODEL), row),
        out_shape=jax.ShapeDtypeStruct((m, D_MODEL), F32),
        scratch_shapes=[pltpu.VMEM((tm, D_MODEL), BF16), pltpu.VMEM((tm, D_MODEL), F32)],
        compiler_params=_cparams(("parallel", "arbitrary")),
        name="outproj_mlp",
    )(ma, mb, mc, x, wo_all, gpm, gpre, wup_all, wdn_all, gpost)


def _log_mult_a(d):
    c = np.zeros(d.shape, np.int64)
    for window, dil in A_BRANCHES:
        c += ((d >= 0) & (d <= window) & (d % dil == 0)).astype(np.int64)
    return np.where(c > 0, np.log(np.maximum(c, 1)), NEG).astype(np.float32)


def _log_mult_c(d):
    return np.where((d >= 0) & (d <= C_WINDOW), 0.0, NEG).astype(np.float32)


def _tables(log_mult, d):
    return jnp.asarray(np.maximum(d, 0).astype(np.float32)), jnp.asarray(log_mult(d))


def _prompt_tables_a(seq):
    d = np.arange(TQ)[:, None] - np.arange(seq)[None, :] + (seq - TQ)
    return _tables(_log_mult_a, d)


def _prompt_tables_c():
    i = np.arange(TQ)[:, None]
    j = np.arange(2 * TQ)[None, :]
    return _tables(_log_mult_c, np.stack([TQ + i - j, i - j]))


def _sample_tables(log_mult, past, rows):
    d_c = past + np.arange(rows)[:, None] - np.arange(past)[None, :]
    d_n = np.arange(rows)[:, None] - np.arange(LANES)[None, :]
    d_n = np.where(np.arange(LANES)[None, :] < rows, d_n, -1)
    return _tables(log_mult, d_c) + _tables(log_mult, d_n)


def _head_masks(rows):
    left = lax.broadcasted_iota(jnp.int32, (rows, LANES), 1) < HEAD_DIM
    return left, (lambda x: jnp.where(left, x, 0.0), lambda x: jnp.where(left, 0.0, x))


def _stack_heads(q, pick):
    return jnp.concatenate([pick[0](q), pick[1](q)], axis=0)


def _softmax_pv(s, v, extra=None, base2=False):
    ex = jnp.exp2 if base2 else jnp.exp
    m = jnp.max(s, axis=-1, keepdims=True)
    pr = ex(s - m)
    l = jnp.sum(pr, axis=-1, keepdims=True)
    if extra is not None:
        l = l + ex(extra - m)
    return jnp.dot(pr.astype(BF16), v, preferred_element_type=F32) / l


def _attn_a_prompt_kernel(slope_ref, q_ref, k_ref, v_ref, d_ref, l_ref, o_ref, bias_scr, k_scr, v_scr, *,
                          n_cls):
    p = pl.program_id(1)
    c = pl.program_id(2)
    seq = k_ref.shape[0]
    left, pick = _head_masks(TQ)

    @pl.when(c == 0)
    def _():
        for h in range(2):
            bias_scr[h * TQ:(h + 1) * TQ, :] = (l_ref[...] - slope_ref[p, h] * d_ref[...]) * LOG2E
        k_scr[...] = k_ref[...].astype(BF16)
        v_scr[...] = v_ref[...].astype(BF16)

    for cc in range(n_cls):
        @pl.when(c == cc)
        def _(cc=cc):
            keys = [TQ * (cc * A_CLASS_BLOCKS + j + 1) for j in range(A_CLASS_BLOCKS)]

            def scores(j):
                qq = _stack_heads(q_ref[j * TQ:(j + 1) * TQ, :] * (ATTN_SCALE * LOG2E), pick).astype(BF16)
                s = lax.dot_general(qq, k_scr[0:keys[j], :], _NT, preferred_element_type=F32)
                return s + bias_scr[:, seq - keys[j]:seq]

            ahead = 2
            s = {j: scores(j) for j in range(min(ahead, A_CLASS_BLOCKS))}
            for j in range(A_CLASS_BLOCKS):
                o = _softmax_pv(s.pop(j), v_scr[0:keys[j], :], base2=True)
                o_ref[j * TQ:(j + 1) * TQ, :] = jnp.where(left, o[:TQ], o[TQ:]).astype(o_ref.dtype)
                if j + ahead < A_CLASS_BLOCKS:
                    s[j + ahead] = scores(j + ahead)


def _attn_a_prompt(z, slopes, tables, *, batch, seq):
    pairs = A_HEADS // 2
    rows = TQ * A_CLASS_BLOCKS
    n_cls = seq // rows
    dtab, ltab = tables
    const2 = lambda b, p, c: (0, 0)
    kern = functools.partial(_attn_a_prompt_kernel, n_cls=n_cls)
    return pl.pallas_call(
        kern,
        grid=(batch, pairs, n_cls),
        in_specs=[
            pl.BlockSpec(memory_space=pltpu.SMEM),
            pl.BlockSpec((rows, LANES), lambda b, p, c: (b * n_cls + c, Z_AQ // LANES + p)),
            pl.BlockSpec((seq, LANES), lambda b, p, c: (b, Z_AK // LANES + p)),
            pl.BlockSpec((seq, LANES), lambda b, p, c: (b, Z_AV // LANES + p)),
            pl.BlockSpec(dtab.shape, const2),
            pl.BlockSpec(ltab.shape, const2),
        ],
        out_specs=pl.BlockSpec((rows, LANES), lambda b, p, c: (b * n_cls + c, p)),
        out_shape=jax.ShapeDtypeStruct((batch * seq, A_WIDTH), BF16),
        scratch_shapes=[pltpu.VMEM((2 * TQ, seq), F32), pltpu.VMEM((seq, LANES), BF16),
                        pltpu.VMEM((seq, LANES), BF16)],
        compiler_params=_cparams(("parallel", "parallel", "arbitrary")),
        name="attn_a_prompt",
    )(slopes, z, z, z, dtab, ltab)


def _attn_c_prompt_kernel(slope_ref, sink_ref, q_ref, k_ref, v_ref, d_ref, l_ref, o_ref, *, blocks):
    p = pl.program_id(1)
    c = pl.program_id(2)
    base = c * (blocks * TQ)
    left, pick = _head_masks(TQ)
    first = c == 0
    bias = [jnp.concatenate([l_ref[t] - slope_ref[p, h] * d_ref[t] for h in range(2)], axis=0)
            for t in range(2)]
    upper = lax.broadcasted_iota(jnp.int32, (2 * TQ, 1), 0) < TQ
    sink = jnp.where(upper, sink_ref[p, 0], sink_ref[p, 1])
    work = []
    for j in range(blocks):
        start = jnp.maximum(base - TQ, 0) if j == 0 else base + (j - 1) * TQ
        start = pl.multiple_of(start, TQ)
        qq = _stack_heads(q_ref[j * TQ:(j + 1) * TQ, :] * ATTN_SCALE, pick).astype(BF16)
        k = k_ref[pl.ds(start, 2 * TQ), :].astype(BF16)
        b = jnp.where(first, bias[1], bias[0]) if j == 0 else bias[0]
        work.append((j, start, lax.dot_general(qq, k, _NT, preferred_element_type=F32) + b))
    for j, start, s in work:
        o = _softmax_pv(s, v_ref[pl.ds(start, 2 * TQ), :].astype(BF16), extra=sink)
        o_ref[j * TQ:(j + 1) * TQ, :] = jnp.where(left, o[:TQ], o[TQ:]).astype(o_ref.dtype)


def _attn_c_prompt(z, slopes, sinks, tables, *, batch, seq, blocks):
    pairs = C_HEADS // 2
    rows = TQ * blocks
    nchunk = seq // rows
    dtab, ltab = tables
    smem = pl.BlockSpec(memory_space=pltpu.SMEM)
    const3 = lambda b, p, c: (0, 0, 0)
    kern = functools.partial(_attn_c_prompt_kernel, blocks=blocks)
    return pl.pallas_call(
        kern,
        grid=(batch, pairs, nchunk),
        in_specs=[
            smem, smem,
            pl.BlockSpec((rows, LANES), lambda b, p, c: (b * nchunk + c, Z_CQ // LANES + p)),
            pl.BlockSpec((seq, LANES), lambda b, p, c: (b, Z_CK // LANES + p // 3)),
            pl.BlockSpec((seq, LANES), lambda b, p, c: (b, Z_CV // LANES + p // 3)),
            pl.BlockSpec(dtab.shape, const3),
            pl.BlockSpec(ltab.shape, const3),
        ],
        out_specs=pl.BlockSpec((rows, LANES), lambda b, p, c: (b * nchunk + c, p)),
        out_shape=jax.ShapeDtypeStruct((batch * seq, C_WIDTH), BF16),
        compiler_params=_cparams(("parallel", "parallel", "arbitrary")),
        name="attn_c_prompt",
    )(slopes, sinks, z, z, z, dtab, ltab)


def _attn_sample_kernel(sink_ref, q_ref, kn_ref, vn_ref, c_ref, dc_ref, lc_ref, dn_ref, ln_ref, o_ref, *,
                        slopes, kv_of_pair, new_rows, has_sink):
    group, rows = q_ref.shape[:2]
    pairs = len(slopes) // 2
    v_row = c_ref.shape[1] // 2
    left, pick = _head_masks(rows)
    upper = lax.broadcasted_iota(jnp.int32, (2 * rows, 1), 0) < rows
    stacked_bias = lambda p, l_ref, d_ref: jnp.concatenate(
        [l_ref[...] - slopes[2 * p + h] * d_ref[...] for h in range(2)], axis=0)
    feat = lambda p: slice(kv_of_pair(p) * LANES, (kv_of_pair(p) + 1) * LANES)
    bias_c = [stacked_bias(p, lc_ref, dc_ref) for p in range(pairs)]
    bias_new = [stacked_bias(p, ln_ref, dn_ref) for p in range(pairs)]

    work = []
    for n in range(group):
        for p in range(pairs):
            qq = _stack_heads(q_ref[n, :, p * LANES:(p + 1) * LANES] * ATTN_SCALE, pick)
            kc = c_ref[n, feat(p), :].astype(BF16)
            s_c = jnp.dot(qq.astype(BF16), kc, preferred_element_type=F32) + bias_c[p]
            work.append((n, p, qq, s_c))
    for n, p, qq, s_c in work:
        kn = kn_ref[n, :, feat(p)]
        vn = vn_ref[n, :, feat(p)]
        bias_n = bias_new[p]
        s_n = [jnp.sum(qq * kn[j:j + 1, :], axis=-1, keepdims=True) + bias_n[:, j:j + 1]
               for j in range(new_rows)]
        m = jnp.max(s_c, axis=-1, keepdims=True)
        for s in s_n:
            m = jnp.maximum(m, s)
        p_c = jnp.exp(s_c - m)
        l = jnp.sum(p_c, axis=-1, keepdims=True)
        vc = c_ref[n, v_row + kv_of_pair(p) * LANES:v_row + (kv_of_pair(p) + 1) * LANES, :].astype(BF16)
        o = lax.dot_general(p_c.astype(BF16), vc, _NT, preferred_element_type=F32)
        for j, s in enumerate(s_n):
            p_n = jnp.exp(s - m)
            l = l + p_n
            o = o + p_n * vn[j:j + 1, :]
        if has_sink:
            l = l + jnp.exp(jnp.where(upper, sink_ref[p, 0], sink_ref[p, 1]) - m)
        o = o / l
        o_ref[n, :, p * LANES:(p + 1) * LANES] = jnp.where(left, o[:rows], o[rows:])


def _attn_sample(z3, cache_t, slopes, sinks, tables, *, layer, group, q_col, k_col, v_col, kv_width,
                 kv_of_pair, new_rows, has_sink, name):
    n, rows, _ = z3.shape
    feats, past = cache_t.shape[2:]
    width = len(slopes) // 2 * LANES
    const2 = lambda b: (0, 0)
    zblk = lambda col, w: pl.BlockSpec((group, rows, w), lambda b: (b, 0, col // w))
    kern = functools.partial(_attn_sample_kernel, slopes=tuple(float(s) for s in slopes),
                             kv_of_pair=kv_of_pair, new_rows=new_rows, has_sink=has_sink)
    return pl.pallas_call(
        kern,
        grid=(n // group,),
        in_specs=[
            pl.BlockSpec(memory_space=pltpu.SMEM),
            zblk(q_col, width), zblk(k_col, kv_width), zblk(v_col, kv_width),
            pl.BlockSpec((None, group, feats, past), lambda b: (layer, b, 0, 0)),
        ] + [pl.BlockSpec(t.shape, const2) for t in tables],
        out_specs=pl.BlockSpec((group, rows, width), lambda b: (b, 0, 0)),
        out_shape=jax.ShapeDtypeStruct((n, rows, width), F32),
        compiler_params=_cparams(("arbitrary",)),
        name=name,
    )(sinks, z3, z3, z3, cache_t, *tables)


def _bf16_pieces(a):
    p1 = a.astype(BF16)
    r1 = a - p1.astype(F32)
    p2 = r1.astype(BF16)
    return p1, p2, (r1 - p2.astype(F32)).astype(BF16)


def _mm1(a, b, dims=_NN):
    return lax.dot_general(a.astype(BF16), b.astype(BF16), dims, preferred_element_type=F32)


def _unit_lower_inverse(mats, row, col):
    blk = lambda s: (row >> s) == (col >> s)
    eye = jnp.where(row == col, 1.0, 0.0)
    pw = [jnp.where(blk(4), -a, 0.0) for a in mats]
    t = [eye + x for x in pw]
    for _ in range(3):
        pw = [_mm1(x, x).astype(BF16) for x in pw]
        t = [x + _mm1(x, y) for x, y in zip(t, pw)]
    for s in (4, 5, 6):
        off = [jnp.where(blk(s + 1), jnp.where(blk(s), 0.0, a), 0.0).astype(BF16) for a in mats]
        tb = [x.astype(BF16) for x in t]
        mid = [_mm1(o, y) for o, y in zip(off, tb)]
        t = [x - _mm1(y, m) for x, y, m in zip(t, tb, mid)]
    return t


def _softplus(x):
    return jnp.maximum(x, 0.0) + jnp.log(1.0 + jnp.exp(-jnp.abs(x)))


def _lane_pick(x, idx):
    lane = lax.broadcasted_iota(jnp.int32, x.shape, 1)
    return jnp.sum(jnp.where(lane == idx, x, 0.0), axis=-1, keepdims=True)


def _delta_prompt_kernel(alog_ref, dtb_ref, bq_ref, bk_ref, bv_ref, bz_ref, ba_ref, cw_ref, cb_ref, s0_ref,
                         dng_ref, o_ref, s_ref, ext_scr, st_scr):
    c = pl.program_id(1)
    group, ch = bq_ref.shape[:2]
    heads = range(B_HEADS)
    chains = [(n, h) for n in range(group) for h in heads]
    every = range(len(chains))

    @pl.when(c == 0)
    def _():
        ext_scr[:, :, 0:SUBLANES, :] = jnp.zeros((group, 3, SUBLANES, B_WIDTH), F32)
        for n in range(group):
            for i in range(3):
                ext_scr[n, i, SUBLANES - (CONV_W - 1):SUBLANES, :] = cb_ref[n, :, i * B_WIDTH:(i + 1) * B_WIDTH]
        st_scr[...] = s0_ref[...]

    conv = []
    for n in range(group):
        per_seq = []
        for i, ref in enumerate((bq_ref, bk_ref, bv_ref)):
            ext_scr[n, i, SUBLANES:SUBLANES + ch, :] = ref[n]
            acc = jnp.zeros((ch, B_WIDTH), F32)
            for j in range(CONV_W):
                w = cw_ref[j:j + 1, i * B_WIDTH:(i + 1) * B_WIDTH]
                acc = acc + ext_scr[n, i, pl.ds(SUBLANES - (CONV_W - 1) + j, ch), :] * w
            ext_scr[n, i, 0:SUBLANES, :] = ext_scr[n, i, ch:ch + SUBLANES, :]
            per_seq.append(acc * _sigmoid(acc))
        conv.append(per_seq)

    row = lax.broadcasted_iota(jnp.int32, (ch, ch), 0)
    col = lax.broadcasted_iota(jnp.int32, (ch, ch), 1)
    tri = jnp.where(row >= col, 1.0, 0.0).astype(BF16)
    dg = lambda x, y: lax.dot_general(x, y, _NN, preferred_element_type=F32)

    sl = [slice(h * B_HEAD_DIM, (h + 1) * B_HEAD_DIM) for h in heads]
    unit = lambda x: x * lax.rsqrt(jnp.sum(x * x, axis=-1, keepdims=True) + EPS)
    q = [unit(conv[n][0][:, sl[h]]) * (B_HEAD_DIM ** -0.5) for n, h in chains]
    k = [unit(conv[n][1][:, sl[h]]) for n, h in chains]
    v = [conv[n][2][:, sl[h]] for n, h in chains]
    pick = jnp.concatenate([jnp.where(row == c, 1.0, 0.0).astype(BF16) for c in range(2 * B_HEADS)], axis=1)
    spread = [sum(dg(piece, pick) for piece in _bf16_pieces(ba_ref[n])) for n in range(group)]
    beta = [_sigmoid(spread[n][:, sl[h]]) for n, h in chains]

    gc = []
    for n, h in chains:
        logit = spread[n][:, (B_HEADS + h) * LANES:(B_HEADS + h + 1) * LANES]
        g = -jnp.exp(jnp.full((1, 1), alog_ref[h], F32)) * _softplus(logit + dtb_ref[h])
        gc.append(sum(dg(tri, piece) for piece in _bf16_pieces(g)))

    decay = [jnp.exp(jnp.where(row >= col, x - x.T, NEG)) for x in gc]
    egc = [jnp.exp(x) for x in gc]
    kb = [k[i] * beta[i] for i in every]
    vb = [v[i] * beta[i] for i in every]
    akk = [_mm1(kb[i], k[i], _NT) for i in every]
    qk = [_mm1(q[i], k[i], _NT) for i in every]
    a = [jnp.where(row > col, akk[i] * decay[i], 0.0) for i in every]
    attn = [qk[i] * decay[i] for i in every]
    t = _unit_lower_inverse(a, row, col)
    u = [_mm1(t[i], vb[i]) for i in every]
    w = [_mm1(t[i], kb[i] * egc[i]) for i in every]

    gl = [x[ch - 1:ch, :] for x in gc]
    st = [st_scr[n, h] for n, h in chains]
    w_s = [_mm1(w[i], st[i]) for i in every]
    q_s = [_mm1(q[i] * egc[i], st[i]) for i in every]
    v_new = [u[i] - w_s[i] for i in every]
    a_v = [_mm1(attn[i], v_new[i]) for i in every]
    kd_t = [(k[i] * jnp.exp(gl[i] - gc[i])).T for i in every]
    k_v = [_mm1(kd_t[i], v_new[i]) for i in every]
    for i, (n, h) in enumerate(chains):
        st_scr[n, h] = st[i] * jnp.exp(gl[i]) + k_v[i]
        zz = bz_ref[n, :, sl[h]]
        o_ref[n, :, sl[h]] = (_rms(q_s[i] + a_v[i], dng_ref[...]) * (zz * _sigmoid(zz))).astype(o_ref.dtype)

    @pl.when(c == pl.num_programs(1) - 1)
    def _():
        s_ref[...] = st_scr[...]


def _delta_prompt(z3, conv_w, conv0, s0, a_log, dt_bias, dn_g, *, group, chunk):
    assert chunk == B_HEAD_DIM
    batch, seq, _ = z3.shape
    nc = seq // chunk
    smem = pl.BlockSpec(memory_space=pltpu.SMEM)
    state = (B_HEADS, B_HEAD_DIM, B_HEAD_DIM)
    zb = lambda col, w: pl.BlockSpec((group, chunk, w), lambda n, c: (n, c, col // w))
    return pl.pallas_call(
        _delta_prompt_kernel,
        grid=(batch // group, nc),
        in_specs=[
            smem, smem,
            zb(Z_BQ, B_WIDTH), zb(Z_BK, B_WIDTH), zb(Z_BV, B_WIDTH), zb(Z_BZ, B_WIDTH), zb(Z_BA, LANES),
            pl.BlockSpec(conv_w.shape, lambda n, c: (0, 0)),
            pl.BlockSpec((group, CONV_W - 1, 3 * B_WIDTH), lambda n, c: (n, 0, 0)),
            pl.BlockSpec((group,) + state, lambda n, c: (n, 0, 0, 0)),
            pl.BlockSpec((1, B_HEAD_DIM), lambda n, c: (0, 0)),
        ],
        out_specs=[
            pl.BlockSpec((group, chunk, B_WIDTH), lambda n, c: (n, c, 0)),
            pl.BlockSpec((group,) + state, lambda n, c: (n, 0, 0, 0)),
        ],
        out_shape=[
            jax.ShapeDtypeStruct((batch, seq, B_WIDTH), BF16),
            jax.ShapeDtypeStruct((batch,) + state, F32),
        ],
        scratch_shapes=[
            pltpu.VMEM((group, 3, chunk + SUBLANES, B_WIDTH), F32),
            pltpu.VMEM((group,) + state, F32),
        ],
        compiler_params=_cparams(("parallel", "arbitrary")),
        name="delta_prompt",
    )(a_log, dt_bias, z3, z3, z3, z3, z3, conv_w, conv0, s0, dn_g)


def _delta_sample_kernel(alog_ref, dtb_ref, bq_ref, bk_ref, bv_ref, bz_ref, ba_ref, cw_ref, cb_ref, s0_ref,
                         dng_ref, o_ref, s_ref, *, new_rows):
    d = B_HEAD_DIM
    group = bq_ref.shape[0]
    chains = [(n, h) for n in range(group) for h in range(B_HEADS)]

    def conv_rows(n, x_ref, i):
        x = x_ref[n]
        cb = cb_ref[n]
        seg = slice(i * B_WIDTH, (i + 1) * B_WIDTH)
        ext = [cb[r:r + 1, seg] for r in range(CONV_W - 1)] + [x[t:t + 1, :] for t in range(new_rows)]
        out = []
        for t in range(new_rows):
            acc = ext[t] * cw_ref[0:1, seg]
            for j in range(1, CONV_W):
                acc = acc + ext[t + j] * cw_ref[j:j + 1, seg]
            out.append(acc * _sigmoid(acc))
        return out

    qs = [conv_rows(n, bq_ref, 0) for n in range(group)]
    ks = [conv_rows(n, bk_ref, 1) for n in range(group)]
    vs = [conv_rows(n, bv_ref, 2) for n in range(group)]
    sl = [slice(h * d, (h + 1) * d) for h in range(B_HEADS)]
    beta = [_sigmoid(_lane_pick(ba_ref[n], h)) for n, h in chains]
    g = [-jnp.exp(jnp.full((1, 1), alog_ref[h], F32)) * _softplus(_lane_pick(ba_ref[n], B_HEADS + h) + dtb_ref[h])
         for n, h in chains]
    eye = lax.broadcasted_iota(jnp.int32, (d, d), 0) == lax.broadcasted_iota(jnp.int32, (d, d), 1)
    to_col = lambda r: jnp.sum(jnp.where(eye, r, 0.0), axis=-1, keepdims=True)
    unit = lambda r: r * lax.rsqrt(jnp.sum(r * r, axis=-1, keepdims=True) + EPS)

    st = [s0_ref[n, h] for n, h in chains]
    o_ref[...] = jnp.zeros(o_ref.shape, F32)
    for t in range(new_rows):
        for i, (n, h) in enumerate(chains):
            a = jnp.exp(g[i][t:t + 1, :])
            kc = to_col(unit(ks[n][t][:, sl[h]]))
            k_s = jnp.sum(kc * st[i], axis=0, keepdims=True)
            st[i] = a * st[i] + kc * (beta[i][t:t + 1, :] * (vs[n][t][:, sl[h]] - a * k_s))
            qc = to_col(unit(qs[n][t][:, sl[h]]) * (d ** -0.5))
            o = jnp.sum(qc * st[i], axis=0, keepdims=True)
            zz = bz_ref[n, t:t + 1, sl[h]]
            o_ref[n, t:t + 1, sl[h]] = _rms(o, dng_ref[...]) * (zz * _sigmoid(zz))
    for i, (n, h) in enumerate(chains):
        s_ref[n, h] = st[i]


def _delta_sample(z3, conv_w, conv_buf, s0, a_log, dt_bias, dn_g, *, layer, group, new_rows):
    n, rows, _ = z3.shape
    smem = pl.BlockSpec(memory_space=pltpu.SMEM)
    zb = lambda col: pl.BlockSpec((group, rows, B_WIDTH), lambda b: (b, 0, col // B_WIDTH))
    state = (B_HEADS, B_HEAD_DIM, B_HEAD_DIM)
    kern = functools.partial(_delta_sample_kernel, new_rows=new_rows)
    return pl.pallas_call(
        kern,
        grid=(n // group,),
        in_specs=[
            smem, smem,
            zb(Z_BQ), zb(Z_BK), zb(Z_BV), zb(Z_BZ),
            pl.BlockSpec((group, rows, LANES), lambda b: (b, 0, Z_BA // LANES)),
            pl.BlockSpec(conv_w.shape, lambda b: (0, 0)),
            pl.BlockSpec((None, group, CONV_W - 1, 3 * B_WIDTH), lambda b: (layer, b, 0, 0)),
            pl.BlockSpec((None, group) + state, lambda b: (layer, b, 0, 0, 0)),
            pl.BlockSpec((1, B_HEAD_DIM), lambda b: (0, 0)),
        ],
        out_specs=[
            pl.BlockSpec((group, rows, B_WIDTH), lambda b: (b, 0, 0)),
            pl.BlockSpec((group,) + state, lambda b: (b, 0, 0, 0)),
        ],
        out_shape=[
            jax.ShapeDtypeStruct((n, rows, B_WIDTH), F32),
            jax.ShapeDtypeStruct((n,) + state, F32),
        ],
        compiler_params=_cparams(("arbitrary",)),
        name="delta_sample",
    )(a_log, dt_bias, z3, z3, z3, z3, z3, conv_w, conv_buf, s0, dn_g)


def _pair_order(rows):
    shape = rows.shape
    rows = rows.reshape(shape[0], 2, 2, 3, *shape[2:])
    return jnp.swapaxes(rows, 2, 3).reshape(shape)


def _layout_w_in(w_in):
    wt = jnp.swapaxes(w_in, 1, 2)
    cq = _pair_order(wt[:, O_CQ:O_CKV].reshape(DEPTH, C_HEADS, HEAD_DIM, D_MODEL)).reshape(DEPTH, C_WIDTH, D_MODEL)
    pad = lambda n: jnp.zeros((DEPTH, n, D_MODEL), w_in.dtype)
    parts = [wt[:, :O_BQKV], cq, wt[:, O_BQKV:O_BETA], pad(Z_CK - Z_BZ - B_WIDTH),
             wt[:, O_CKV:O_END], wt[:, O_BETA:O_CQ], pad(Z_COLS - Z_BA - 2 * B_HEADS)]
    return jnp.concatenate(parts, axis=1).astype(BF16)


def _layout_w_out(w_out):
    c_rows = _pair_order(w_out[:, A_WIDTH + B_WIDTH:].reshape(DEPTH, C_HEADS, HEAD_DIM, D_MODEL))
    c_rows = c_rows.reshape(DEPTH, C_WIDTH, D_MODEL)
    return jnp.concatenate([w_out[:, :A_WIDTH + B_WIDTH], c_rows], axis=1).astype(BF16)


def _pair_table(v):
    return jnp.asarray(v, F32).reshape(-1, 2)


def _feature_major(cache):
    depth, n, tokens = cache.shape[:3]
    return jnp.transpose(cache, (0, 1, 3, 4, 5, 2)).reshape(depth, n, -1, tokens)


def kernel(x_prompt, x_sample, cache_dilated_kv, cache_swa_kv, state_delta_s, state_delta_conv, g_pre_mix,
           w_in, delta_conv_w, delta_a_log, delta_dt_bias, delta_norm_g, swa_sinks, w_out, g_post_mix,
           g_pre_mlp, w_up, w_down, g_post_mlp):
    batch, seq, _ = x_prompt.shape
    dec_batch, dec_seq, _ = x_sample.shape
    past_a = cache_dilated_kv.shape[2]
    past_c = cache_swa_kv.shape[2]
    assert seq % (TQ * A_CLASS_BLOCKS) == 0 and seq == A_WINDOW_MAX and dec_seq <= SAMPLE_ROWS
    assert past_a == A_WINDOW_MAX and past_c == C_WINDOW

    w_in_p = _layout_w_in(w_in)
    w_out_p = _layout_w_out(w_out)
    w_up_b = w_up.astype(BF16)
    w_dn_b = w_down.astype(BF16)
    cache_a = _feature_major(cache_dilated_kv)
    cache_c = _feature_major(cache_swa_kv)

    tab_pa = _prompt_tables_a(seq)
    tab_pc = _prompt_tables_c()
    tab_sa = _sample_tables(_log_mult_a, past_a, SAMPLE_ROWS)
    tab_sc = _sample_tables(_log_mult_c, past_c, SAMPLE_ROWS)
    alibi_a = _alibi(A_HEADS)
    alibi_c = _alibi(C_HEADS)[np.asarray(C_HEAD_ORDER)]
    slopes_a = _pair_table(alibi_a)
    slopes_c = _pair_table(alibi_c)
    no_sinks = jnp.zeros_like(slopes_a)
    c_order = np.asarray(C_HEAD_ORDER)

    conv0 = jnp.zeros((batch, CONV_W - 1, 3 * B_WIDTH), F32)
    s_zero = jnp.zeros((batch, B_HEADS, B_HEAD_DIM, B_HEAD_DIM), F32)

    xp = x_prompt.reshape(batch * seq, D_MODEL)
    xs = jnp.pad(x_sample, ((0, 0), (0, SAMPLE_ROWS - dec_seq), (0, 0))).reshape(dec_batch * SAMPLE_ROWS, D_MODEL)
    ms = xs.shape[0]

    p_s, p_conv = [], []
    s_akv, s_ckv, s_s, s_conv = [], [], [], []
    kv_t = tuple(jnp.zeros((DEPTH, batch, w, seq), F32) for w in (2 * A_WIDTH, 2 * C_KV_WIDTH))
    for l in range(DEPTH):
        row = lambda a: a[l].reshape(1, -1)
        sinks_c = _pair_table(swa_sinks[l][c_order])
        zp, kv_t = _inproj(xp, row(g_pre_mix), w_in_p, l, 1024, kv_t=kv_t, seq=seq)
        zs, _ = _inproj(xs, row(g_pre_mix), w_in_p, l, ms)
        zs3 = zs.reshape(dec_batch, SAMPLE_ROWS, Z_COLS)

        a_p = _attn_a_prompt(zp, slopes_a, tab_pa, batch=batch, seq=seq)
        c_p = _attn_c_prompt(zp, slopes_c, sinks_c, tab_pc, batch=batch, seq=seq, blocks=8)
        zp3 = zp.reshape(batch, seq, Z_COLS)
        b_p, st_p = _delta_prompt(zp3, delta_conv_w[l], conv0, s_zero, delta_a_log[l], delta_dt_bias[l],
                                  row(delta_norm_g), group=batch, chunk=B_HEAD_DIM)
        b_p = b_p.reshape(batch * seq, B_WIDTH)

        a_s = _attn_sample(zs3, cache_a, alibi_a, no_sinks, tab_sa, layer=l, group=1, q_col=Z_AQ, k_col=Z_AK,
                           v_col=Z_AV, kv_width=A_WIDTH, kv_of_pair=lambda p: p, new_rows=dec_seq,
                           has_sink=False, name="attn_a_sample")
        c_s = _attn_sample(zs3, cache_c, alibi_c, sinks_c, tab_sc, layer=l, group=1, q_col=Z_CQ, k_col=Z_CK,
                           v_col=Z_CV, kv_width=C_KV_WIDTH, kv_of_pair=lambda p: p // 3, new_rows=dec_seq,
                           has_sink=True, name="attn_c_sample")
        b_s, st_s = _delta_sample(zs3, delta_conv_w[l], state_delta_conv, state_delta_s, delta_a_log[l],
                                  delta_dt_bias[l], row(delta_norm_g), layer=l, group=1, new_rows=dec_seq)

        mlp_w = (w_out_p, row(g_post_mix), row(g_pre_mlp), w_up_b, w_dn_b, row(g_post_mlp))
        xp = _mlp(a_p, b_p, c_p, xp, *mlp_w, l, 512, 512)
        xs = _mlp(a_s.reshape(ms, A_WIDTH), b_s.reshape(ms, B_WIDTH), c_s.reshape(ms, C_WIDTH), xs,
                  *mlp_w, l, ms, 512)

        p_s.append(st_p)
        p_conv.append(zp3[:, seq - (CONV_W - 1):, Z_BQ:Z_BQ + 3 * B_WIDTH])
        s_akv.append(zs3[:, :dec_seq, Z_AK:Z_AK + 2 * A_WIDTH].reshape(dec_batch, dec_seq, 2, A_HEADS, HEAD_DIM))
        s_ckv.append(zs3[:, :dec_seq, Z_CK:Z_CK + 2 * C_KV_WIDTH]
                     .reshape(dec_batch, dec_seq, 2, C_KV_HEADS, HEAD_DIM))
        s_s.append(st_s)
        full = jnp.concatenate([state_delta_conv[l], zs3[:, :dec_seq, Z_BQ:Z_BQ + 3 * B_WIDTH]], axis=1)
        s_conv.append(full[:, full.shape[1] - (CONV_W - 1):])

    yp = xp.reshape(batch, seq, D_MODEL)
    ys = xs.reshape(dec_batch, SAMPLE_ROWS, D_MODEL)[:, :dec_seq]
    token_major = lambda t, heads: jnp.transpose(
        t.reshape(DEPTH, batch, 2, heads, HEAD_DIM, t.shape[-1]), (0, 1, 5, 2, 3, 4))
    p_akv = token_major(kv_t[0][..., seq - A_WINDOW_MAX:], A_HEADS)
    p_ckv = token_major(kv_t[1][..., seq - C_WINDOW:], C_KV_HEADS)
    return (yp, ys, p_akv, p_ckv, jnp.stack(p_s), jnp.stack(p_conv),
            jnp.stack(s_akv), jnp.stack(s_ckv), jnp.stack(s_s), jnp.stack(s_conv))
```

```python
import functools

import numpy as np
import jax
import jax.numpy as jnp
from jax import lax
from jax.experimental import pallas as pl
from jax.experimental.pallas import tpu as pltpu

F32 = jnp.float32
BF16 = jnp.bfloat16

D_MODEL = 2048
DEPTH = 4
HEAD_DIM = 64
A_HEADS = 12
A_BRANCHES = ((128, 1), (512, 4), (2048, 16))
A_WINDOW_MAX = 2048
B_HEAD_DIM = 128
B_HEADS = 4
CONV_W = 4
C_HEADS = 12
C_KV_HEADS = 4
C_WINDOW = 128
A_WIDTH = A_HEADS * HEAD_DIM
B_WIDTH = B_HEADS * B_HEAD_DIM
C_WIDTH = C_HEADS * HEAD_DIM
C_KV_WIDTH = C_KV_HEADS * HEAD_DIM
D_FF = 4 * D_MODEL
EPS = 1e-6
ATTN_SCALE = HEAD_DIM ** -0.5
LOG2E = 1.4426950408889634
NEG = -1e30

LANES = 128
SUBLANES = 8
VMEM_LIMIT = 56 * 1024 * 1024

Z_TILE = 768
Z_AQ, Z_AK, Z_AV = 0, 768, 1536
Z_CQ = 2304
Z_BQ, Z_BK, Z_BV, Z_BZ = 3072, 3584, 4096, 4608
Z_CK, Z_CV = 5376, 5632
Z_BA = 5888
Z_COLS = 6144
O_BQKV, O_BZ, O_BETA, O_CQ, O_CKV, O_END = 2304, 3840, 4352, 4360, 5128, 5640
C_HEAD_ORDER = (0, 3, 1, 4, 2, 5, 6, 9, 7, 10, 8, 11)
SAMPLE_ROWS = 8
TQ = 128
A_CLASS_BLOCKS = 16

_NN = (((1,), (0,)), ((), ()))
_NT = (((1,), (1,)), ((), ()))


def _alibi(n):
    return np.asarray([2.0 ** (-8.0 * (i + 1) / n) for i in range(n)], dtype=np.float32)


def _cparams(sem):
    return pltpu.CompilerParams(dimension_semantics=sem, vmem_limit_bytes=VMEM_LIMIT)


def _rms(x, g):
    return x * lax.rsqrt(jnp.mean(x * x, axis=-1, keepdims=True) + EPS) * g


def _sigmoid(x):
    return 1.0 / (1.0 + jnp.exp(-x))


def _inproj_kernel(x_ref, g_ref, w_ref, *refs, feature_major):
    j = pl.program_id(1)
    h_scr = refs[-1]
    z_ref = refs[-4] if feature_major else refs[-2]

    @pl.when(j == 0)
    def _():
        h_scr[...] = _rms(x_ref[...], g_ref[...]).astype(BF16)

    z_ref[...] = lax.dot_general(h_scr[...], w_ref[...], _NT, preferred_element_type=F32)

    if feature_major:
        ta_ref, tc_ref = refs[-3], refs[-2]

        @pl.when((j == Z_AK // Z_TILE) | (j == Z_AV // Z_TILE))
        def _():
            ta_ref[...] = z_ref[...].T

        @pl.when(j == Z_CK // Z_TILE)
        def _():
            tc_ref[...] = z_ref[:, 0:2 * C_KV_WIDTH].T


def _inproj(x, g, w_all, layer, tm, kv_t=None, seq=None):
    m = x.shape[0]
    in_specs = [
        pl.BlockSpec((tm, D_MODEL), lambda i, j: (i, 0)),
        pl.BlockSpec((1, D_MODEL), lambda i, j: (0, 0)),
        pl.BlockSpec((None, Z_TILE, D_MODEL), lambda i, j: (layer, j, 0)),
    ]
    out_specs = [pl.BlockSpec((tm, Z_TILE), lambda i, j: (i, j))]
    out_shape = [jax.ShapeDtypeStruct((m, Z_COLS), F32)]
    args, aliases = [x, g, w_all], {}
    if seq is not None:
        tps = seq // tm
        a_tile = lambda j: jnp.clip(j - Z_AK // Z_TILE, 0, 1)
        out_specs += [
            pl.BlockSpec((None, None, Z_TILE, tm), lambda i, j: (layer, i // tps, a_tile(j), i % tps)),
            pl.BlockSpec((None, None, 2 * C_KV_WIDTH, tm), lambda i, j: (layer, i // tps, 0, i % tps)),
        ]
        out_shape += [jax.ShapeDtypeStruct(t.shape, t.dtype) for t in kv_t]
        in_specs += [pl.BlockSpec(memory_space=pl.ANY)] * 2
        aliases = {3: 1, 4: 2}
        args += list(kv_t)
    out = pl.pallas_call(
        functools.partial(_inproj_kernel, feature_major=seq is not None),
        grid=(m // tm, Z_COLS // Z_TILE),
        in_specs=in_specs,
        out_specs=out_specs,
        out_shape=out_shape,
        input_output_aliases=aliases,
        scratch_shapes=[pltpu.VMEM((tm, D_MODEL), BF16)],
        compiler_params=_cparams(("parallel", "arbitrary")),
        name="inproj",
    )(*args)
    return out[0], tuple(out[1:])


def _mlp_kernel(ma_ref, mb_ref, mc_ref, x_ref, wo_ref, gpm_ref, gpre_ref, wup_ref, wdn_ref, gpost_ref,
                y_ref, hm_scr, acc_scr):
    f = pl.program_id(1)

    @pl.when(f == 0)
    def _():
        y = jnp.dot(ma_ref[...].astype(BF16), wo_ref[0:A_WIDTH, :], preferred_element_type=F32)
        y += jnp.dot(mb_ref[...].astype(BF16), wo_ref[A_WIDTH:A_WIDTH + B_WIDTH, :],
                     preferred_element_type=F32)
        y += jnp.dot(mc_ref[...].astype(BF16), wo_ref[A_WIDTH + B_WIDTH:, :], preferred_element_type=F32)
        x1 = x_ref[...] + _rms(y, gpm_ref[...])
        y_ref[...] = x1
        hm_scr[...] = _rms(x1, gpre_ref[...]).astype(BF16)
        acc_scr[...] = jnp.zeros_like(acc_scr)

    u = jnp.dot(hm_scr[...], wup_ref[...], preferred_element_type=F32)
    u = jnp.square(jnp.maximum(u, 0.0)).astype(BF16)
    acc_scr[...] += jnp.dot(u, wdn_ref[...], preferred_element_type=F32)

    @pl.when(f == pl.num_programs(1) - 1)
    def _():
        y_ref[...] += _rms(acc_scr[...], gpost_ref[...])


def _mlp(ma, mb, mc, x, wo_all, gpm, gpre, wup_all, wdn_all, gpost, layer, tm, tf):
    m = x.shape[0]
    row = lambda i, f: (i, 0)
    vec = pl.BlockSpec((1, D_MODEL), lambda i, f: (0, 0))
    return pl.pallas_call(
        _mlp_kernel,
        grid=(m // tm, D_FF // tf),
        in_specs=[
            pl.BlockSpec((tm, A_WIDTH), row),
            pl.BlockSpec((tm, B_WIDTH), row),
            pl.BlockSpec((tm, C_WIDTH), row),
            pl.BlockSpec((tm, D_MODEL), row),
            pl.BlockSpec((None, D_MODEL, D_MODEL), lambda i, f: (layer, 0, 0), pipeline_mode=pl.Buffered(1)),
            vec, vec,
            pl.BlockSpec((None, D_MODEL, tf), lambda i, f: (layer, 0, f)),
            pl.BlockSpec((None, tf, D_MODEL), lambda i, f: (layer, f, 0)),
            vec,
        ],
        out_specs=pl.BlockSpec((tm, D_MODEL), row),
        out_shape=jax.ShapeDtypeStruct((m, D_MODEL), F32),
        scratch_shapes=[pltpu.VMEM((tm, D_MODEL), BF16), pltpu.VMEM((tm, D_MODEL), F32)],
        compiler_params=_cparams(("parallel", "arbitrary")),
        name="outproj_mlp",
    )(ma, mb, mc, x, wo_all, gpm, gpre, wup_all, wdn_all, gpost)


def _log_mult_a(d):
    c = np.zeros(d.shape, np.int64)
    for window, dil in A_BRANCHES:
        c += ((d >= 0) & (d <= window) & (d % dil == 0)).astype(np.int64)
    return np.where(c > 0, np.log(np.maximum(c, 1)), NEG).astype(np.float32)


def _log_mult_c(d):
    return np.where((d >= 0) & (d <= C_WINDOW), 0.0, NEG).astype(np.float32)


def _tables(log_mult, d):
    return jnp.asarray(np.maximum(d, 0).astype(np.float32)), jnp.asarray(log_mult(d))


def _prompt_tables_a(seq):
    d = np.arange(TQ)[:, None] - np.arange(seq)[None, :] + (seq - TQ)
    return _tables(_log_mult_a, d)


def _prompt_tables_c():
    i = np.arange(TQ)[:, None]
    j = np.arange(2 * TQ)[None, :]
    return _tables(_log_mult_c, np.stack([TQ + i - j, i - j]))


def _sample_tables(log_mult, past, rows):
    d_c = past + np.arange(rows)[:, None] - np.arange(past)[None, :]
    d_n = np.arange(rows)[:, None] - np.arange(LANES)[None, :]
    d_n = np.where(np.arange(LANES)[None, :] < rows, d_n, -1)
    return _tables(log_mult, d_c) + _tables(log_mult, d_n)


def _head_masks(rows):
    left = lax.broadcasted_iota(jnp.int32, (rows, LANES), 1) < HEAD_DIM
    return left, (lambda x: jnp.where(left, x, 0.0), lambda x: jnp.where(left, 0.0, x))


def _stack_heads(q, pick):
    return jnp.concatenate([pick[0](q), pick[1](q)], axis=0)


def _softmax_pv(s, v, extra=None, base2=False):
    ex = jnp.exp2 if base2 else jnp.exp
    m = jnp.max(s, axis=-1, keepdims=True)
    pr = ex(s - m)
    l = jnp.sum(pr, axis=-1, keepdims=True)
    if extra is not None:
        l = l + ex(extra - m)
    return jnp.dot(pr.astype(BF16), v, preferred_element_type=F32) / l


def _attn_a_prompt_kernel(slope_ref, q_ref, k_ref, v_ref, d_ref, l_ref, o_ref, bias_scr, k_scr, v_scr, *,
                          n_cls):
    p = pl.program_id(1)
    c = pl.program_id(2)
    seq = k_ref.shape[0]
    left, pick = _head_masks(TQ)

    @pl.when(c == 0)
    def _():
        for h in range(2):
            bias_scr[h * TQ:(h + 1) * TQ, :] = (l_ref[...] - slope_ref[p, h] * d_ref[...]) * LOG2E
        k_scr[...] = k_ref[...].astype(BF16)
        v_scr[...] = v_ref[...].astype(BF16)

    for cc in range(n_cls):
        @pl.when(c == cc)
        def _(cc=cc):
            keys = [TQ * (cc * A_CLASS_BLOCKS + j + 1) for j in range(A_CLASS_BLOCKS)]

            def scores(j):
                qq = _stack_heads(q_ref[j * TQ:(j + 1) * TQ, :] * (ATTN_SCALE * LOG2E), pick).astype(BF16)
                s = lax.dot_general(qq, k_scr[0:keys[j], :], _NT, preferred_element_type=F32)
                return s + bias_scr[:, seq - keys[j]:seq]

            ahead = 3
            s = {j: scores(j) for j in range(min(ahead, A_CLASS_BLOCKS))}
            for j in range(A_CLASS_BLOCKS):
                o = _softmax_pv(s.pop(j), v_scr[0:keys[j], :], base2=True)
                o_ref[j * TQ:(j + 1) * TQ, :] = jnp.where(left, o[:TQ], o[TQ:]).astype(o_ref.dtype)
                if j + ahead < A_CLASS_BLOCKS:
                    s[j + ahead] = scores(j + ahead)


def _attn_a_prompt(z, slopes, tables, *, batch, seq):
    pairs = A_HEADS // 2
    rows = TQ * A_CLASS_BLOCKS
    n_cls = seq // rows
    dtab, ltab = tables
    const2 = lambda b, p, c: (0, 0)
    kern = functools.partial(_attn_a_prompt_kernel, n_cls=n_cls)
    return pl.pallas_call(
        kern,
        grid=(batch, pairs, n_cls),
        in_specs=[
            pl.BlockSpec(memory_space=pltpu.SMEM),
            pl.BlockSpec((rows, LANES), lambda b, p, c: (b * n_cls + c, Z_AQ // LANES + p)),
            pl.BlockSpec((seq, LANES), lambda b, p, c: (b, Z_AK // LANES + p)),
            pl.BlockSpec((seq, LANES), lambda b, p, c: (b, Z_AV // LANES + p)),
            pl.BlockSpec(dtab.shape, const2),
            pl.BlockSpec(ltab.shape, const2),
        ],
        out_specs=pl.BlockSpec((rows, LANES), lambda b, p, c: (b * n_cls + c, p)),
        out_shape=jax.ShapeDtypeStruct((batch * seq, A_WIDTH), BF16),
        scratch_shapes=[pltpu.VMEM((2 * TQ, seq), F32), pltpu.VMEM((seq, LANES), BF16),
                        pltpu.VMEM((seq, LANES), BF16)],
        compiler_params=_cparams(("parallel", "parallel", "arbitrary")),
        name="attn_a_prompt",
    )(slopes, z, z, z, dtab, ltab)


def _attn_c_prompt_kernel(slope_ref, sink_ref, q_ref, k_ref, v_ref, d_ref, l_ref, o_ref, *, blocks):
    p = pl.program_id(1)
    c = pl.program_id(2)
    base = c * (blocks * TQ)
    left, pick = _head_masks(TQ)
    first = c == 0
    bias = [jnp.concatenate([l_ref[t] - slope_ref[p, h] * d_ref[t] for h in range(2)], axis=0)
            for t in range(2)]
    upper = lax.broadcasted_iota(jnp.int32, (2 * TQ, 1), 0) < TQ
    sink = jnp.where(upper, sink_ref[p, 0], sink_ref[p, 1])
    work = []
    for j in range(blocks):
        start = jnp.maximum(base - TQ, 0) if j == 0 else base + (j - 1) * TQ
        start = pl.multiple_of(start, TQ)
        qq = _stack_heads(q_ref[j * TQ:(j + 1) * TQ, :] * ATTN_SCALE, pick).astype(BF16)
        k = k_ref[pl.ds(start, 2 * TQ), :].astype(BF16)
        b = jnp.where(first, bias[1], bias[0]) if j == 0 else bias[0]
        work.append((j, start, lax.dot_general(qq, k, _NT, preferred_element_type=F32) + b))
    for j, start, s in work:
        o = _softmax_pv(s, v_ref[pl.ds(start, 2 * TQ), :].astype(BF16), extra=sink)
        o_ref[j * TQ:(j + 1) * TQ, :] = jnp.where(left, o[:TQ], o[TQ:]).astype(o_ref.dtype)


def _attn_c_prompt(z, slopes, sinks, tables, *, batch, seq, blocks):
    pairs = C_HEADS // 2
    rows = TQ * blocks
    nchunk = seq // rows
    dtab, ltab = tables
    smem = pl.BlockSpec(memory_space=pltpu.SMEM)
    const3 = lambda b, p, c: (0, 0, 0)
    kern = functools.partial(_attn_c_prompt_kernel, blocks=blocks)
    return pl.pallas_call(
        kern,
        grid=(batch, pairs, nchunk),
        in_specs=[
            smem, smem,
            pl.BlockSpec((rows, LANES), lambda b, p, c: (b * nchunk + c, Z_CQ // LANES + p)),
            pl.BlockSpec((seq, LANES), lambda b, p, c: (b, Z_CK // LANES + p // 3)),
            pl.BlockSpec((seq, LANES), lambda b, p, c: (b, Z_CV // LANES + p // 3)),
            pl.BlockSpec(dtab.shape, const3),
            pl.BlockSpec(ltab.shape, const3),
        ],
        out_specs=pl.BlockSpec((rows, LANES), lambda b, p, c: (b * nchunk + c, p)),
        out_shape=jax.ShapeDtypeStruct((batch * seq, C_WIDTH), BF16),
        compiler_params=_cparams(("parallel", "parallel", "arbitrary")),
        name="attn_c_prompt",
    )(slopes, sinks, z, z, z, dtab, ltab)


def _attn_sample_kernel(sink_ref, q_ref, kn_ref, vn_ref, c_ref, dc_ref, lc_ref, dn_ref, ln_ref, o_ref, *,
                        slopes, kv_of_pair, new_rows, has_sink):
    group, rows = q_ref.shape[:2]
    pairs = len(slopes) // 2
    v_row = c_ref.shape[1] // 2
    left, pick = _head_masks(rows)
    upper = lax.broadcasted_iota(jnp.int32, (2 * rows, 1), 0) < rows
    stacked_bias = lambda p, l_ref, d_ref: jnp.concatenate(
        [l_ref[...] - slopes[2 * p + h] * d_ref[...] for h in range(2)], axis=0)
    feat = lambda p: slice(kv_of_pair(p) * LANES, (kv_of_pair(p) + 1) * LANES)
    bias_c = [stacked_bias(p, lc_ref, dc_ref) for p in range(pairs)]
    bias_new = [stacked_bias(p, ln_ref, dn_ref) for p in range(pairs)]

    work = []
    for n in range(group):
        for p in range(pairs):
            qq = _stack_heads(q_ref[n, :, p * LANES:(p + 1) * LANES] * ATTN_SCALE, pick)
            kc = c_ref[n, feat(p), :].astype(BF16)
            s_c = jnp.dot(qq.astype(BF16), kc, preferred_element_type=F32) + bias_c[p]
            work.append((n, p, qq, s_c))
    for n, p, qq, s_c in work:
        kn = kn_ref[n, :, feat(p)]
        vn = vn_ref[n, :, feat(p)]
        bias_n = bias_new[p]
        s_n = [jnp.sum(qq * kn[j:j + 1, :], axis=-1, keepdims=True) + bias_n[:, j:j + 1]
               for j in range(new_rows)]
        m = jnp.max(s_c, axis=-1, keepdims=True)
        for s in s_n:
            m = jnp.maximum(m, s)
        p_c = jnp.exp(s_c - m)
        l = jnp.sum(p_c, axis=-1, keepdims=True)
        vc = c_ref[n, v_row + kv_of_pair(p) * LANES:v_row + (kv_of_pair(p) + 1) * LANES, :].astype(BF16)
        o = lax.dot_general(p_c.astype(BF16), vc, _NT, preferred_element_type=F32)
        for j, s in enumerate(s_n):
            p_n = jnp.exp(s - m)
            l = l + p_n
            o = o + p_n * vn[j:j + 1, :]
        if has_sink:
            l = l + jnp.exp(jnp.where(upper, sink_ref[p, 0], sink_ref[p, 1]) - m)
        o = o / l
        o_ref[n, :, p * LANES:(p + 1) * LANES] = jnp.where(left, o[:rows], o[rows:])


def _attn_sample(z3, cache_t, slopes, sinks, tables, *, layer, group, q_col, k_col, v_col, kv_width,
                 kv_of_pair, new_rows, has_sink, name):
    n, rows, _ = z3.shape
    feats, past = cache_t.shape[2:]
    width = len(slopes) // 2 * LANES
    const2 = lambda b: (0, 0)
    zblk = lambda col, w: pl.BlockSpec((group, rows, w), lambda b: (b, 0, col // w))
    kern = functools.partial(_attn_sample_kernel, slopes=tuple(float(s) for s in slopes),
                             kv_of_pair=kv_of_pair, new_rows=new_rows, has_sink=has_sink)
    return pl.pallas_call(
        kern,
        grid=(n // group,),
        in_specs=[
            pl.BlockSpec(memory_space=pltpu.SMEM),
            zblk(q_col, width), zblk(k_col, kv_width), zblk(v_col, kv_width),
            pl.BlockSpec((None, group, feats, past), lambda b: (layer, b, 0, 0)),
        ] + [pl.BlockSpec(t.shape, const2) for t in tables],
        out_specs=pl.BlockSpec((group, rows, width), lambda b: (b, 0, 0)),
        out_shape=jax.ShapeDtypeStruct((n, rows, width), F32),
        compiler_params=_cparams(("arbitrary",)),
        name=name,
    )(sinks, z3, z3, z3, cache_t, *tables)


def _bf16_pieces(a):
    p1 = a.astype(BF16)
    r1 = a - p1.astype(F32)
    p2 = r1.astype(BF16)
    return p1, p2, (r1 - p2.astype(F32)).astype(BF16)


def _mm1(a, b, dims=_NN):
    return lax.dot_general(a.astype(BF16), b.astype(BF16), dims, preferred_element_type=F32)


def _unit_lower_inverse(mats, row, col):
    blk = lambda s: (row >> s) == (col >> s)
    eye = jnp.where(row == col, 1.0, 0.0)
    pw = [jnp.where(blk(4), -a, 0.0) for a in mats]
    t = [eye + x for x in pw]
    for _ in range(3):
        pw = [_mm1(x, x).astype(BF16) for x in pw]
        t = [x + _mm1(x, y) for x, y in zip(t, pw)]
    for s in (4, 5, 6):
        off = [jnp.where(blk(s + 1), jnp.where(blk(s), 0.0, a), 0.0).astype(BF16) for a in mats]
        tb = [x.astype(BF16) for x in t]
        mid = [_mm1(o, y) for o, y in zip(off, tb)]
        t = [x - _mm1(y, m) for x, y, m in zip(t, tb, mid)]
    return t


def _softplus(x):
    return jnp.maximum(x, 0.0) + jnp.log(1.0 + jnp.exp(-jnp.abs(x)))


def _lane_pick(x, idx):
    lane = lax.broadcasted_iota(jnp.int32, x.shape, 1)
    return jnp.sum(jnp.where(lane == idx, x, 0.0), axis=-1, keepdims=True)


def _delta_prompt_kernel(alog_ref, dtb_ref, bq_ref, bk_ref, bv_ref, bz_ref, ba_ref, cw_ref, cb_ref, s0_ref,
                         dng_ref, o_ref, s_ref, ext_scr, st_scr):
    c = pl.program_id(1)
    group, ch = bq_ref.shape[:2]
    heads = range(B_HEADS)
    chains = [(n, h) for n in range(group) for h in heads]
    every = range(len(chains))

    @pl.when(c == 0)
    def _():
        ext_scr[:, :, 0:SUBLANES, :] = jnp.zeros((group, 3, SUBLANES, B_WIDTH), F32)
        for n in range(group):
            for i in range(3):
                ext_scr[n, i, SUBLANES - (CONV_W - 1):SUBLANES, :] = cb_ref[n, :, i * B_WIDTH:(i + 1) * B_WIDTH]
        st_scr[...] = s0_ref[...]

    conv = []
    for n in range(group):
        per_seq = []
        for i, ref in enumerate((bq_ref, bk_ref, bv_ref)):
            ext_scr[n, i, SUBLANES:SUBLANES + ch, :] = ref[n]
            acc = jnp.zeros((ch, B_WIDTH), F32)
            for j in range(CONV_W):
                w = cw_ref[j:j + 1, i * B_WIDTH:(i + 1) * B_WIDTH]
                acc = acc + ext_scr[n, i, pl.ds(SUBLANES - (CONV_W - 1) + j, ch), :] * w
            ext_scr[n, i, 0:SUBLANES, :] = ext_scr[n, i, ch:ch + SUBLANES, :]
            per_seq.append(acc * _sigmoid(acc))
        conv.append(per_seq)

    row = lax.broadcasted_iota(jnp.int32, (ch, ch), 0)
    col = lax.broadcasted_iota(jnp.int32, (ch, ch), 1)
    tri = jnp.where(row >= col, 1.0, 0.0).astype(BF16)
    dg = lambda x, y: lax.dot_general(x, y, _NN, preferred_element_type=F32)

    sl = [slice(h * B_HEAD_DIM, (h + 1) * B_HEAD_DIM) for h in heads]
    unit = lambda x: x * lax.rsqrt(jnp.sum(x * x, axis=-1, keepdims=True) + EPS)
    q = [unit(conv[n][0][:, sl[h]]) * (B_HEAD_DIM ** -0.5) for n, h in chains]
    k = [unit(conv[n][1][:, sl[h]]) for n, h in chains]
    v = [conv[n][2][:, sl[h]] for n, h in chains]
    pick = jnp.concatenate([jnp.where(row == c, 1.0, 0.0).astype(BF16) for c in range(2 * B_HEADS)], axis=1)
    spread = [sum(dg(piece, pick) for piece in _bf16_pieces(ba_ref[n])) for n in range(group)]
    beta = [_sigmoid(spread[n][:, sl[h]]) for n, h in chains]

    gc = []
    for n, h in chains:
        logit = spread[n][:, (B_HEADS + h) * LANES:(B_HEADS + h + 1) * LANES]
        g = -jnp.exp(jnp.full((1, 1), alog_ref[h], F32)) * _softplus(logit + dtb_ref[h])
        gc.append(sum(dg(tri, piece) for piece in _bf16_pieces(g)))

    decay = [jnp.exp(jnp.where(row >= col, x - x.T, NEG)) for x in gc]
    egc = [jnp.exp(x) for x in gc]
    kb = [k[i] * beta[i] for i in every]
    vb = [v[i] * beta[i] for i in every]
    akk = [_mm1(kb[i], k[i], _NT) for i in every]
    qk = [_mm1(q[i], k[i], _NT) for i in every]
    a = [jnp.where(row > col, akk[i] * decay[i], 0.0) for i in every]
    attn = [qk[i] * decay[i] for i in every]
    t = _unit_lower_inverse(a, row, col)
    u = [_mm1(t[i], vb[i]) for i in every]
    w = [_mm1(t[i], kb[i] * egc[i]) for i in every]

    gl = [x[ch - 1:ch, :] for x in gc]
    st = [st_scr[n, h] for n, h in chains]
    w_s = [_mm1(w[i], st[i]) for i in every]
    q_s = [_mm1(q[i] * egc[i], st[i]) for i in every]
    v_new = [u[i] - w_s[i] for i in every]
    a_v = [_mm1(attn[i], v_new[i]) for i in every]
    kd_t = [(k[i] * jnp.exp(gl[i] - gc[i])).T for i in every]
    k_v = [_mm1(kd_t[i], v_new[i]) for i in every]
    for i, (n, h) in enumerate(chains):
        st_scr[n, h] = st[i] * jnp.exp(gl[i]) + k_v[i]
        zz = bz_ref[n, :, sl[h]]
        o_ref[n, :, sl[h]] = (_rms(q_s[i] + a_v[i], dng_ref[...]) * (zz * _sigmoid(zz))).astype(o_ref.dtype)

    @pl.when(c == pl.num_programs(1) - 1)
    def _():
        s_ref[...] = st_scr[...]


def _delta_prompt(z3, conv_w, conv0, s0, a_log, dt_bias, dn_g, *, group, chunk):
    assert chunk == B_HEAD_DIM
    batch, seq, _ = z3.shape
    nc = seq // chunk
    smem = pl.BlockSpec(memory_space=pltpu.SMEM)
    state = (B_HEADS, B_HEAD_DIM, B_HEAD_DIM)
    zb = lambda col, w: pl.BlockSpec((group, chunk, w), lambda n, c: (n, c, col // w))
    return pl.pallas_call(
        _delta_prompt_kernel,
        grid=(batch // group, nc),
        in_specs=[
            smem, smem,
            zb(Z_BQ, B_WIDTH), zb(Z_BK, B_WIDTH), zb(Z_BV, B_WIDTH), zb(Z_BZ, B_WIDTH), zb(Z_BA, LANES),
            pl.BlockSpec(conv_w.shape, lambda n, c: (0, 0)),
            pl.BlockSpec((group, CONV_W - 1, 3 * B_WIDTH), lambda n, c: (n, 0, 0)),
            pl.BlockSpec((group,) + state, lambda n, c: (n, 0, 0, 0)),
            pl.BlockSpec((1, B_HEAD_DIM), lambda n, c: (0, 0)),
        ],
        out_specs=[
            pl.BlockSpec((group, chunk, B_WIDTH), lambda n, c: (n, c, 0)),
            pl.BlockSpec((group,) + state, lambda n, c: (n, 0, 0, 0)),
        ],
        out_shape=[
            jax.ShapeDtypeStruct((batch, seq, B_WIDTH), BF16),
            jax.ShapeDtypeStruct((batch,) + state, F32),
        ],
        scratch_shapes=[
            pltpu.VMEM((group, 3, chunk + SUBLANES, B_WIDTH), F32),
            pltpu.VMEM((group,) + state, F32),
        ],
        compiler_params=_cparams(("parallel", "arbitrary")),
        name="delta_prompt",
    )(a_log, dt_bias, z3, z3, z3, z3, z3, conv_w, conv0, s0, dn_g)


def _delta_sample_kernel(alog_ref, dtb_ref, bq_ref, bk_ref, bv_ref, bz_ref, ba_ref, cw_ref, cb_ref, s0_ref,
                         dng_ref, o_ref, s_ref, *, new_rows):
    d = B_HEAD_DIM
    group = bq_ref.shape[0]
    chains = [(n, h) for n in range(group) for h in range(B_HEADS)]

    def conv_rows(n, x_ref, i):
        x = x_ref[n]
        cb = cb_ref[n]
        seg = slice(i * B_WIDTH, (i + 1) * B_WIDTH)
        ext = [cb[r:r + 1, seg] for r in range(CONV_W - 1)] + [x[t:t + 1, :] for t in range(new_rows)]
        out = []
        for t in range(new_rows):
            acc = ext[t] * cw_ref[0:1, seg]
            for j in range(1, CONV_W):
                acc = acc + ext[t + j] * cw_ref[j:j + 1, seg]
            out.append(acc * _sigmoid(acc))
        return out

    qs = [conv_rows(n, bq_ref, 0) for n in range(group)]
    ks = [conv_rows(n, bk_ref, 1) for n in range(group)]
    vs = [conv_rows(n, bv_ref, 2) for n in range(group)]
    sl = [slice(h * d, (h + 1) * d) for h in range(B_HEADS)]
    beta = [_sigmoid(_lane_pick(ba_ref[n], h)) for n, h in chains]
    g = [-jnp.exp(jnp.full((1, 1), alog_ref[h], F32)) * _softplus(_lane_pick(ba_ref[n], B_HEADS + h) + dtb_ref[h])
         for n, h in chains]
    eye = lax.broadcasted_iota(jnp.int32, (d, d), 0) == lax.broadcasted_iota(jnp.int32, (d, d), 1)
    to_col = lambda r: jnp.sum(jnp.where(eye, r, 0.0), axis=-1, keepdims=True)
    unit = lambda r: r * lax.rsqrt(jnp.sum(r * r, axis=-1, keepdims=True) + EPS)

    st = [s0_ref[n, h] for n, h in chains]
    o_ref[...] = jnp.zeros(o_ref.shape, F32)
    for t in range(new_rows):
        for i, (n, h) in enumerate(chains):
            a = jnp.exp(g[i][t:t + 1, :])
            kc = to_col(unit(ks[n][t][:, sl[h]]))
            k_s = jnp.sum(kc * st[i], axis=0, keepdims=True)
            st[i] = a * st[i] + kc * (beta[i][t:t + 1, :] * (vs[n][t][:, sl[h]] - a * k_s))
            qc = to_col(unit(qs[n][t][:, sl[h]]) * (d ** -0.5))
            o = jnp.sum(qc * st[i], axis=0, keepdims=True)
            zz = bz_ref[n, t:t + 1, sl[h]]
            o_ref[n, t:t + 1, sl[h]] = _rms(o, dng_ref[...]) * (zz * _sigmoid(zz))
    for i, (n, h) in enumerate(chains):
        s_ref[n, h] = st[i]


def _delta_sample(z3, conv_w, conv_buf, s0, a_log, dt_bias, dn_g, *, layer, group, new_rows):
    n, rows, _ = z3.shape
    smem = pl.BlockSpec(memory_space=pltpu.SMEM)
    zb = lambda col: pl.BlockSpec((group, rows, B_WIDTH), lambda b: (b, 0, col // B_WIDTH))
    state = (B_HEADS, B_HEAD_DIM, B_HEAD_DIM)
    kern = functools.partial(_delta_sample_kernel, new_rows=new_rows)
    return pl.pallas_call(
        kern,
        grid=(n // group,),
        in_specs=[
            smem, smem,
            zb(Z_BQ), zb(Z_BK), zb(Z_BV), zb(Z_BZ),
            pl.BlockSpec((group, rows, LANES), lambda b: (b, 0, Z_BA // LANES)),
            pl.BlockSpec(conv_w.shape, lambda b: (0, 0)),
            pl.BlockSpec((None, group, CONV_W - 1, 3 * B_WIDTH), lambda b: (layer, b, 0, 0)),
            pl.BlockSpec((None, group) + state, lambda b: (layer, b, 0, 0, 0)),
            pl.BlockSpec((1, B_HEAD_DIM), lambda b: (0, 0)),
        ],
        out_specs=[
            pl.BlockSpec((group, rows, B_WIDTH), lambda b: (b, 0, 0)),
            pl.BlockSpec((group,) + state, lambda b: (b, 0, 0, 0)),
        ],
        out_shape=[
            jax.ShapeDtypeStruct((n, rows, B_WIDTH), F32),
            jax.ShapeDtypeStruct((n,) + state, F32),
        ],
        compiler_params=_cparams(("arbitrary",)),
        name="delta_sample",
    )(a_log, dt_bias, z3, z3, z3, z3, z3, conv_w, conv_buf, s0, dn_g)


def _pair_order(rows):
    shape = rows.shape
    rows = rows.reshape(shape[0], 2, 2, 3, *shape[2:])
    return jnp.swapaxes(rows, 2, 3).reshape(shape)


def _layout_w_in(w_in):
    wt = jnp.swapaxes(w_in, 1, 2)
    cq = _pair_order(wt[:, O_CQ:O_CKV].reshape(DEPTH, C_HEADS, HEAD_DIM, D_MODEL)).reshape(DEPTH, C_WIDTH, D_MODEL)
    pad = lambda n: jnp.zeros((DEPTH, n, D_MODEL), w_in.dtype)
    parts = [wt[:, :O_BQKV], cq, wt[:, O_BQKV:O_BETA], pad(Z_CK - Z_BZ - B_WIDTH),
             wt[:, O_CKV:O_END], wt[:, O_BETA:O_CQ], pad(Z_COLS - Z_BA - 2 * B_HEADS)]
    return jnp.concatenate(parts, axis=1).astype(BF16)


def _layout_w_out(w_out):
    c_rows = _pair_order(w_out[:, A_WIDTH + B_WIDTH:].reshape(DEPTH, C_HEADS, HEAD_DIM, D_MODEL))
    c_rows = c_rows.reshape(DEPTH, C_WIDTH, D_MODEL)
    return jnp.concatenate([w_out[:, :A_WIDTH + B_WIDTH], c_rows], axis=1).astype(BF16)


def _pair_table(v):
    return jnp.asarray(v, F32).reshape(-1, 2)


def _feature_major(cache):
    depth, n, tokens = cache.shape[:3]
    return jnp.transpose(cache, (0, 1, 3, 4, 5, 2)).reshape(depth, n, -1, tokens)


def kernel(x_prompt, x_sample, cache_dilated_kv, cache_swa_kv, state_delta_s, state_delta_conv, g_pre_mix,
           w_in, delta_conv_w, delta_a_log, delta_dt_bias, delta_norm_g, swa_sinks, w_out, g_post_mix,
           g_pre_mlp, w_up, w_down, g_post_mlp):
    batch, seq, _ = x_prompt.shape
    dec_batch, dec_seq, _ = x_sample.shape
    past_a = cache_dilated_kv.shape[2]
    past_c = cache_swa_kv.shape[2]
    assert seq % (TQ * A_CLASS_BLOCKS) == 0 and seq == A_WINDOW_MAX and dec_seq <= SAMPLE_ROWS
    assert past_a == A_WINDOW_MAX and past_c == C_WINDOW

    w_in_p = _layout_w_in(w_in)
    w_out_p = _layout_w_out(w_out)
    w_up_b = w_up.astype(BF16)
    w_dn_b = w_down.astype(BF16)
    cache_a = _feature_major(cache_dilated_kv)
    cache_c = _feature_major(cache_swa_kv)

    tab_pa = _prompt_tables_a(seq)
    tab_pc = _prompt_tables_c()
    tab_sa = _sample_tables(_log_mult_a, past_a, SAMPLE_ROWS)
    tab_sc = _sample_tables(_log_mult_c, past_c, SAMPLE_ROWS)
    alibi_a = _alibi(A_HEADS)
    alibi_c = _alibi(C_HEADS)[np.asarray(C_HEAD_ORDER)]
    slopes_a = _pair_table(alibi_a)
    slopes_c = _pair_table(alibi_c)
    no_sinks = jnp.zeros_like(slopes_a)
    c_order = np.asarray(C_HEAD_ORDER)

    conv0 = jnp.zeros((batch, CONV_W - 1, 3 * B_WIDTH), F32)
    s_zero = jnp.zeros((batch, B_HEADS, B_HEAD_DIM, B_HEAD_DIM), F32)

    xp = x_prompt.reshape(batch * seq, D_MODEL)
    xs = jnp.pad(x_sample, ((0, 0), (0, SAMPLE_ROWS - dec_seq), (0, 0))).reshape(dec_batch * SAMPLE_ROWS, D_MODEL)
    ms = xs.shape[0]

    p_s, p_conv = [], []
    s_akv, s_ckv, s_s, s_conv = [], [], [], []
    kv_t = tuple(jnp.zeros((DEPTH, batch, w, seq), F32) for w in (2 * A_WIDTH, 2 * C_KV_WIDTH))
    for l in range(DEPTH):
        row = lambda a: a[l].reshape(1, -1)
        sinks_c = _pair_table(swa_sinks[l][c_order])
        zp, kv_t = _inproj(xp, row(g_pre_mix), w_in_p, l, 1024, kv_t=kv_t, seq=seq)
        zs, _ = _inproj(xs, row(g_pre_mix), w_in_p, l, ms)
        zs3 = zs.reshape(dec_batch, SAMPLE_ROWS, Z_COLS)

        a_p = _attn_a_prompt(zp, slopes_a, tab_pa, batch=batch, seq=seq)
        c_p = _attn_c_prompt(zp, slopes_c, sinks_c, tab_pc, batch=batch, seq=seq, blocks=16)
        zp3 = zp.reshape(batch, seq, Z_COLS)
        b_p, st_p = _delta_prompt(zp3, delta_conv_w[l], conv0, s_zero, delta_a_log[l], delta_dt_bias[l],
                                  row(delta_norm_g), group=batch, chunk=B_HEAD_DIM)
        b_p = b_p.reshape(batch * seq, B_WIDTH)

        a_s = _attn_sample(zs3, cache_a, alibi_a, no_sinks, tab_sa, layer=l, group=1, q_col=Z_AQ, k_col=Z_AK,
                           v_col=Z_AV, kv_width=A_WIDTH, kv_of_pair=lambda p: p, new_rows=dec_seq,
                           has_sink=False, name="attn_a_sample")
        c_s = _attn_sample(zs3, cache_c, alibi_c, sinks_c, tab_sc, layer=l, group=1, q_col=Z_CQ, k_col=Z_CK,
                           v_col=Z_CV, kv_width=C_KV_WIDTH, kv_of_pair=lambda p: p // 3, new_rows=dec_seq,
                           has_sink=True, name="attn_c_sample")
        b_s, st_s = _delta_sample(zs3, delta_conv_w[l], state_delta_conv, state_delta_s, delta_a_log[l],
                                  delta_dt_bias[l], row(delta_norm_g), layer=l, group=1, new_rows=dec_seq)

        mlp_w = (w_out_p, row(g_post_mix), row(g_pre_mlp), w_up_b, w_dn_b, row(g_post_mlp))
        xp = _mlp(a_p, b_p, c_p, xp, *mlp_w, l, 512, 512)
        xs = _mlp(a_s.reshape(ms, A_WIDTH), b_s.reshape(ms, B_WIDTH), c_s.reshape(ms, C_WIDTH), xs,
                  *mlp_w, l, ms, 1024)

        p_s.append(st_p)
        p_conv.append(zp3[:, seq - (CONV_W - 1):, Z_BQ:Z_BQ + 3 * B_WIDTH])
        s_akv.append(zs3[:, :dec_seq, Z_AK:Z_AK + 2 * A_WIDTH].reshape(dec_batch, dec_seq, 2, A_HEADS, HEAD_DIM))
        s_ckv.append(zs3[:, :dec_seq, Z_CK:Z_CK + 2 * C_KV_WIDTH]
                     .reshape(dec_batch, dec_seq, 2, C_KV_HEADS, HEAD_DIM))
        s_s.append(st_s)
        full = jnp.concatenate([state_delta_conv[l], zs3[:, :dec_seq, Z_BQ:Z_BQ + 3 * B_WIDTH]], axis=1)
        s_conv.append(full[:, full.shape[1] - (CONV_W - 1):])

    yp = xp.reshape(batch, seq, D_MODEL)
    ys = xs.reshape(dec_batch, SAMPLE_ROWS, D_MODEL)[:, :dec_seq]
    token_major = lambda t, heads: jnp.transpose(
        t.reshape(DEPTH, batch, 2, heads, HEAD_DIM, t.shape[-1]), (0, 1, 5, 2, 3, 4))
    p_akv = token_major(kv_t[0][..., seq - A_WINDOW_MAX:], A_HEADS)
    p_ckv = token_major(kv_t[1][..., seq - C_WINDOW:], C_KV_HEADS)
    return (yp, ys, p_akv, p_ckv, jnp.stack(p_s), jnp.stack(p_conv),
            jnp.stack(s_akv), jnp.stack(s_ckv), jnp.stack(s_s), jnp.stack(s_conv))
```

```python
import functools

import numpy as np
import jax
import jax.numpy as jnp
from jax import lax
from jax.experimental import pallas as pl
from jax.experimental.pallas import tpu as pltpu

F32 = jnp.float32
BF16 = jnp.bfloat16

D_MODEL = 2048
DEPTH = 4
HEAD_DIM = 64
A_HEADS = 12
A_BRANCHES = ((128, 1), (512, 4), (2048, 16))
A_WINDOW_MAX = 2048
B_HEAD_DIM = 128
B_HEADS = 4
CONV_W = 4
C_HEADS = 12
C_KV_HEADS = 4
C_WINDOW = 128
A_WIDTH = A_HEADS * HEAD_DIM
B_WIDTH = B_HEADS * B_HEAD_DIM
C_WIDTH = C_HEADS * HEAD_DIM
C_KV_WIDTH = C_KV_HEADS * HEAD_DIM
D_FF = 4 * D_MODEL
EPS = 1e-6
ATTN_SCALE = HEAD_DIM ** -0.5
LOG2E = 1.4426950408889634
NEG = -1e30

LANES = 128
SUBLANES = 8
VMEM_LIMIT = 56 * 1024 * 1024

Z_TILE = 768
Z_AQ, Z_AK, Z_AV = 0, 768, 1536
Z_CQ = 2304
Z_BQ, Z_BK, Z_BV, Z_BZ = 3072, 3584, 4096, 4608
Z_CK, Z_CV = 5376, 5632
Z_BA = 5888
Z_COLS = 6144
O_BQKV, O_BZ, O_BETA, O_CQ, O_CKV, O_END = 2304, 3840, 4352, 4360, 5128, 5640
C_HEAD_ORDER = (0, 3, 1, 4, 2, 5, 6, 9, 7, 10, 8, 11)
SAMPLE_ROWS = 8
TQ = 128
A_CLASS_BLOCKS = 16

_NN = (((1,), (0,)), ((), ()))
_NT = (((1,), (1,)), ((), ()))


def _alibi(n):
    return np.asarray([2.0 ** (-8.0 * (i + 1) / n) for i in range(n)], dtype=np.float32)


def _cparams(sem):
    return pltpu.CompilerParams(dimension_semantics=sem, vmem_limit_bytes=VMEM_LIMIT)


def _rms(x, g):
    return x * lax.rsqrt(jnp.mean(x * x, axis=-1, keepdims=True) + EPS) * g


def _sigmoid(x):
    return 1.0 / (1.0 + jnp.exp(-x))


def _inproj_kernel(x_ref, g_ref, w_ref, *refs, feature_major):
    j = pl.program_id(1)
    h_scr = refs[-1]
    z_ref = refs[-4] if feature_major else refs[-2]

    @pl.when(j == 0)
    def _():
        h_scr[...] = _rms(x_ref[...], g_ref[...]).astype(BF16)

    z_ref[...] = lax.dot_general(h_scr[...], w_ref[...], _NT, preferred_element_type=F32)

    if feature_major:
        ta_ref, tc_ref = refs[-3], refs[-2]

        @pl.when((j == Z_AK // Z_TILE) | (j == Z_AV // Z_TILE))
        def _():
            ta_ref[...] = z_ref[...].T

        @pl.when(j == Z_CK // Z_TILE)
        def _():
            tc_ref[...] = z_ref[:, 0:2 * C_KV_WIDTH].T


def _inproj(x, g, w_all, layer, tm, kv_t=None, seq=None):
    m = x.shape[0]
    in_specs = [
        pl.BlockSpec((tm, D_MODEL), lambda i, j: (i, 0)),
        pl.BlockSpec((1, D_MODEL), lambda i, j: (0, 0)),
        pl.BlockSpec((None, Z_TILE, D_MODEL), lambda i, j: (layer, j, 0)),
    ]
    out_specs = [pl.BlockSpec((tm, Z_TILE), lambda i, j: (i, j))]
    out_shape = [jax.ShapeDtypeStruct((m, Z_COLS), F32)]
    args, aliases = [x, g, w_all], {}
    if seq is not None:
        tps = seq // tm
        a_tile = lambda j: jnp.clip(j - Z_AK // Z_TILE, 0, 1)
        out_specs += [
            pl.BlockSpec((None, None, Z_TILE, tm), lambda i, j: (layer, i // tps, a_tile(j), i % tps)),
            pl.BlockSpec((None, None, 2 * C_KV_WIDTH, tm), lambda i, j: (layer, i // tps, 0, i % tps)),
        ]
        out_shape += [jax.ShapeDtypeStruct(t.shape, t.dtype) for t in kv_t]
        in_specs += [pl.BlockSpec(memory_space=pl.ANY)] * 2
        aliases = {3: 1, 4: 2}
        args += list(kv_t)
    out = pl.pallas_call(
        functools.partial(_inproj_kernel, feature_major=seq is not None),
        grid=(m // tm, Z_COLS // Z_TILE),
        in_specs=in_specs,
        out_specs=out_specs,
        out_shape=out_shape,
        input_output_aliases=aliases,
        scratch_shapes=[pltpu.VMEM((tm, D_MODEL), BF16)],
        compiler_params=_cparams(("parallel", "arbitrary")),
        name="inproj",
    )(*args)
    return out[0], tuple(out[1:])


def _mlp_kernel(ma_ref, mb_ref, mc_ref, x_ref, wo_ref, gpm_ref, gpre_ref, wup_ref, wdn_ref, gpost_ref,
                y_ref, hm_scr, acc_scr):
    f = pl.program_id(1)

    @pl.when(f == 0)
    def _():
        y = jnp.dot(ma_ref[...].astype(BF16), wo_ref[0:A_WIDTH, :], preferred_element_type=F32)
        y += jnp.dot(mb_ref[...].astype(BF16), wo_ref[A_WIDTH:A_WIDTH + B_WIDTH, :],
                     preferred_element_type=F32)
        y += jnp.dot(mc_ref[...].astype(BF16), wo_ref[A_WIDTH + B_WIDTH:, :], preferred_element_type=F32)
        x1 = x_ref[...] + _rms(y, gpm_ref[...])
        y_ref[...] = x1
        hm_scr[...] = _rms(x1, gpre_ref[...]).astype(BF16)
        acc_scr[...] = jnp.zeros_like(acc_scr)

    u = jnp.dot(hm_scr[...], wup_ref[...], preferred_element_type=F32)
    u = jnp.square(jnp.maximum(u, 0.0)).astype(BF16)
    acc_scr[...] += jnp.dot(u, wdn_ref[...], preferred_element_type=F32)

    @pl.when(f == pl.num_programs(1) - 1)
    def _():
        y_ref[...] += _rms(acc_scr[...], gpost_ref[...])


def _mlp(ma, mb, mc, x, wo_all, gpm, gpre, wup_all, wdn_all, gpost, layer, tm, tf):
    m = x.shape[0]
    row = lambda i, f: (i, 0)
    vec = pl.BlockSpec((1, D_MODEL), lambda i, f: (0, 0))
    return pl.pallas_call(
        _mlp_kernel,
        grid=(m // tm, D_FF // tf),
        in_specs=[
            pl.BlockSpec((tm, A_WIDTH), row),
            pl.BlockSpec((tm, B_WIDTH), row),
            pl.BlockSpec((tm, C_WIDTH), row),
            pl.BlockSpec((tm, D_MODEL), row),
            pl.BlockSpec((None, D_MODEL, D_MODEL), lambda i, f: (layer, 0, 0), pipeline_mode=pl.Buffered(1)),
            vec, vec,
            pl.BlockSpec((None, D_MODEL, tf), lambda i, f: (layer, 0, f)),
            pl.BlockSpec((None, tf, D_MODEL), lambda i, f: (layer, f, 0)),
            vec,
        ],
        out_specs=pl.BlockSpec((tm, D_MODEL), row),
        out_shape=jax.ShapeDtypeStruct((m, D_MODEL), F32),
        scratch_shapes=[pltpu.VMEM((tm, D_MODEL), BF16), pltpu.VMEM((tm, D_MODEL), F32)],
        compiler_params=_cparams(("parallel", "arbitrary")),
        name="outproj_mlp",
    )(ma, mb, mc, x, wo_all, gpm, gpre, wup_all, wdn_all, gpost)


def _log_mult_a(d):
    c = np.zeros(d.shape, np.int64)
    for window, dil in A_BRANCHES:
        c += ((d >= 0) & (d <= window) & (d % dil == 0)).astype(np.int64)
    return np.where(c > 0, np.log(np.maximum(c, 1)), NEG).astype(np.float32)


def _log_mult_c(d):
    return np.where((d >= 0) & (d <= C_WINDOW), 0.0, NEG).astype(np.float32)


def _tables(log_mult, d):
    return jnp.asarray(np.maximum(d, 0).astype(np.float32)), jnp.asarray(log_mult(d))


def _prompt_tables_a(seq):
    d = np.arange(TQ)[:, None] - np.arange(seq)[None, :] + (seq - TQ)
    return _tables(_log_mult_a, d)


def _prompt_tables_c():
    i = np.arange(TQ)[:, None]
    j = np.arange(2 * TQ)[None, :]
    return _tables(_log_mult_c, np.stack([TQ + i - j, i - j]))


def _sample_tables(log_mult, past, rows):
    d_c = past + np.arange(rows)[:, None] - np.arange(past)[None, :]
    d_n = np.arange(rows)[:, None] - np.arange(LANES)[None, :]
    d_n = np.where(np.arange(LANES)[None, :] < rows, d_n, -1)
    return _tables(log_mult, d_c) + _tables(log_mult, d_n)


def _head_masks(rows):
    left = lax.broadcasted_iota(jnp.int32, (rows, LANES), 1) < HEAD_DIM
    return left, (lambda x: jnp.where(left, x, 0.0), lambda x: jnp.where(left, 0.0, x))


def _stack_heads(q, pick):
    return jnp.concatenate([pick[0](q), pick[1](q)], axis=0)


def _softmax_pv(s, v, extra=None, base2=False):
    ex = jnp.exp2 if base2 else jnp.exp
    m = jnp.max(s, axis=-1, keepdims=True)
    pr = ex(s - m)
    l = jnp.sum(pr, axis=-1, keepdims=True)
    if extra is not None:
        l = l + ex(extra - m)
    return jnp.dot(pr.astype(BF16), v, preferred_element_type=F32) / l


def _attn_a_prompt_kernel(slope_ref, q_ref, k_ref, v_ref, d_ref, l_ref, o_ref, bias_scr, k_scr, v_scr, *,
                          n_cls):
    p = pl.program_id(1)
    c = pl.program_id(2)
    seq = k_ref.shape[0]
    left, pick = _head_masks(TQ)

    @pl.when(c == 0)
    def _():
        for h in range(2):
            bias_scr[h * TQ:(h + 1) * TQ, :] = (l_ref[...] - slope_ref[p, h] * d_ref[...]) * LOG2E
        k_scr[...] = k_ref[...].astype(BF16)
        v_scr[...] = v_ref[...].astype(BF16)

    for cc in range(n_cls):
        @pl.when(c == cc)
        def _(cc=cc):
            keys = [TQ * (cc * A_CLASS_BLOCKS + j + 1) for j in range(A_CLASS_BLOCKS)]

            def scores(j):
                qq = _stack_heads(q_ref[j * TQ:(j + 1) * TQ, :] * (ATTN_SCALE * LOG2E), pick).astype(BF16)
                s = lax.dot_general(qq, k_scr[0:keys[j], :], _NT, preferred_element_type=F32)
                return s + bias_scr[:, seq - keys[j]:seq]

            ahead = 3
            s = {j: scores(j) for j in range(min(ahead, A_CLASS_BLOCKS))}
            for j in range(A_CLASS_BLOCKS):
                o = _softmax_pv(s.pop(j), v_scr[0:keys[j], :], base2=True)
                o_ref[j * TQ:(j + 1) * TQ, :] = jnp.where(left, o[:TQ], o[TQ:]).astype(o_ref.dtype)
                if j + ahead < A_CLASS_BLOCKS:
                    s[j + ahead] = scores(j + ahead)


def _attn_a_prompt(z, slopes, tables, *, batch, seq):
    pairs = A_HEADS // 2
    rows = TQ * A_CLASS_BLOCKS
    n_cls = seq // rows
    dtab, ltab = tables
    const2 = lambda b, p, c: (0, 0)
    kern = functools.partial(_attn_a_prompt_kernel, n_cls=n_cls)
    return pl.pallas_call(
        kern,
        grid=(batch, pairs, n_cls),
        in_specs=[
            pl.BlockSpec(memory_space=pltpu.SMEM),
            pl.BlockSpec((rows, LANES), lambda b, p, c: (b * n_cls + c, Z_AQ // LANES + p)),
            pl.BlockSpec((seq, LANES), lambda b, p, c: (b, Z_AK // LANES + p)),
            pl.BlockSpec((seq, LANES), lambda b, p, c: (b, Z_AV // LANES + p)),
            pl.BlockSpec(dtab.shape, const2),
            pl.BlockSpec(ltab.shape, const2),
        ],
        out_specs=pl.BlockSpec((rows, LANES), lambda b, p, c: (b * n_cls + c, p)),
        out_shape=jax.ShapeDtypeStruct((batch * seq, A_WIDTH), BF16),
        scratch_shapes=[pltpu.VMEM((2 * TQ, seq), F32), pltpu.VMEM((seq, LANES), BF16),
                        pltpu.VMEM((seq, LANES), BF16)],
        compiler_params=_cparams(("parallel", "parallel", "arbitrary")),
        name="attn_a_prompt",
    )(slopes, z, z, z, dtab, ltab)


def _attn_c_prompt_kernel(slope_ref, sink_ref, q_ref, k_ref, v_ref, d_ref, l_ref, o_ref, *, blocks):
    p = pl.program_id(1)
    c = pl.program_id(2)
    base = c * (blocks * TQ)
    left, pick = _head_masks(TQ)
    first = c == 0
    bias = [jnp.concatenate([l_ref[t] - slope_ref[p, h] * d_ref[t] for h in range(2)], axis=0)
            for t in range(2)]
    upper = lax.broadcasted_iota(jnp.int32, (2 * TQ, 1), 0) < TQ
    sink = jnp.where(upper, sink_ref[p, 0], sink_ref[p, 1])
    work = []
    for j in range(blocks):
        start = jnp.maximum(base - TQ, 0) if j == 0 else base + (j - 1) * TQ
        start = pl.multiple_of(start, TQ)
        qq = _stack_heads(q_ref[j * TQ:(j + 1) * TQ, :] * ATTN_SCALE, pick).astype(BF16)
        k = k_ref[pl.ds(start, 2 * TQ), :].astype(BF16)
        b = jnp.where(first, bias[1], bias[0]) if j == 0 else bias[0]
        work.append((j, start, lax.dot_general(qq, k, _NT, preferred_element_type=F32) + b))
    for j, start, s in work:
        o = _softmax_pv(s, v_ref[pl.ds(start, 2 * TQ), :].astype(BF16), extra=sink)
        o_ref[j * TQ:(j + 1) * TQ, :] = jnp.where(left, o[:TQ], o[TQ:]).astype(o_ref.dtype)


def _attn_c_prompt(z, slopes, sinks, tables, *, batch, seq, blocks):
    pairs = C_HEADS // 2
    rows = TQ * blocks
    nchunk = seq // rows
    dtab, ltab = tables
    smem = pl.BlockSpec(memory_space=pltpu.SMEM)
    const3 = lambda b, p, c: (0, 0, 0)
    kern = functools.partial(_attn_c_prompt_kernel, blocks=blocks)
    return pl.pallas_call(
        kern,
        grid=(batch, pairs, nchunk),
        in_specs=[
            smem, smem,
            pl.BlockSpec((rows, LANES), lambda b, p, c: (b * nchunk + c, Z_CQ // LANES + p)),
            pl.BlockSpec((seq, LANES), lambda b, p, c: (b, Z_CK // LANES + p // 3)),
            pl.BlockSpec((seq, LANES), lambda b, p, c: (b, Z_CV // LANES + p // 3)),
            pl.BlockSpec(dtab.shape, const3),
            pl.BlockSpec(ltab.shape, const3),
        ],
        out_specs=pl.BlockSpec((rows, LANES), lambda b, p, c: (b * nchunk + c, p)),
        out_shape=jax.ShapeDtypeStruct((batch * seq, C_WIDTH), BF16),
        compiler_params=_cparams(("parallel", "parallel", "arbitrary")),
        name="attn_c_prompt",
    )(slopes, sinks, z, z, z, dtab, ltab)


def _attn_sample_kernel(sink_ref, q_ref, kn_ref, vn_ref, c_ref, dc_ref, lc_ref, dn_ref, ln_ref, o_ref, *,
                        slopes, kv_of_pair, new_rows, has_sink):
    group, rows = q_ref.shape[:2]
    pairs = len(slopes) // 2
    v_row = c_ref.shape[1] // 2
    left, pick = _head_masks(rows)
    upper = lax.broadcasted_iota(jnp.int32, (2 * rows, 1), 0) < rows
    stacked_bias = lambda p, l_ref, d_ref: jnp.concatenate(
        [l_ref[...] - slopes[2 * p + h] * d_ref[...] for h in range(2)], axis=0)
    feat = lambda p: slice(kv_of_pair(p) * LANES, (kv_of_pair(p) + 1) * LANES)
    bias_c = [stacked_bias(p, lc_ref, dc_ref) for p in range(pairs)]
    bias_new = [stacked_bias(p, ln_ref, dn_ref) for p in range(pairs)]

    work = []
    for n in range(group):
        for p in range(pairs):
            qq = _stack_heads(q_ref[n, :, p * LANES:(p + 1) * LANES] * ATTN_SCALE, pick)
            kc = c_ref[n, feat(p), :].astype(BF16)
            s_c = jnp.dot(qq.astype(BF16), kc, preferred_element_type=F32) + bias_c[p]
            work.append((n, p, qq, s_c))
    for n, p, qq, s_c in work:
        kn = kn_ref[n, :, feat(p)]
        vn = vn_ref[n, :, feat(p)]
        bias_n = bias_new[p]
        s_n = [jnp.sum(qq * kn[j:j + 1, :], axis=-1, keepdims=True) + bias_n[:, j:j + 1]
               for j in range(new_rows)]
        m = jnp.max(s_c, axis=-1, keepdims=True)
        for s in s_n:
            m = jnp.maximum(m, s)
        p_c = jnp.exp(s_c - m)
        l = jnp.sum(p_c, axis=-1, keepdims=True)
        vc = c_ref[n, v_row + kv_of_pair(p) * LANES:v_row + (kv_of_pair(p) + 1) * LANES, :].astype(BF16)
        o = lax.dot_general(p_c.astype(BF16), vc, _NT, preferred_element_type=F32)
        for j, s in enumerate(s_n):
            p_n = jnp.exp(s - m)
            l = l + p_n
            o = o + p_n * vn[j:j + 1, :]
        if has_sink:
            l = l + jnp.exp(jnp.where(upper, sink_ref[p, 0], sink_ref[p, 1]) - m)
        o = o / l
        o_ref[n, :, p * LANES:(p + 1) * LANES] = jnp.where(left, o[:rows], o[rows:])


def _attn_sample(z3, cache_t, slopes, sinks, tables, *, layer, group, q_col, k_col, v_col, kv_width,
                 kv_of_pair, new_rows, has_sink, name):
    n, rows, _ = z3.shape
    feats, past = cache_t.shape[2:]
    width = len(slopes) // 2 * LANES
    const2 = lambda b: (0, 0)
    zblk = lambda col, w: pl.BlockSpec((group, rows, w), lambda b: (b, 0, col // w))
    kern = functools.partial(_attn_sample_kernel, slopes=tuple(float(s) for s in slopes),
                             kv_of_pair=kv_of_pair, new_rows=new_rows, has_sink=has_sink)
    return dict(
        kernel=kern,
        in_specs=[
            pl.BlockSpec(memory_space=pltpu.SMEM),
            zblk(q_col, width), zblk(k_col, kv_width), zblk(v_col, kv_width),
            pl.BlockSpec((None, group, feats, past), lambda b: (layer, b, 0, 0)),
        ] + [pl.BlockSpec(t.shape, const2) for t in tables],
        args=[sinks, z3, z3, z3, cache_t, *tables],
        out_specs=[pl.BlockSpec((group, rows, width), lambda b: (b, 0, 0))],
        out_shape=[jax.ShapeDtypeStruct((n, rows, width), F32)],
    )


def _bf16_pieces(a):
    p1 = a.astype(BF16)
    r1 = a - p1.astype(F32)
    p2 = r1.astype(BF16)
    return p1, p2, (r1 - p2.astype(F32)).astype(BF16)


def _mm1(a, b, dims=_NN):
    return lax.dot_general(a.astype(BF16), b.astype(BF16), dims, preferred_element_type=F32)


def _unit_lower_inverse(mats, row, col):
    blk = lambda s: (row >> s) == (col >> s)
    eye = jnp.where(row == col, 1.0, 0.0)
    pw = [jnp.where(blk(4), -a, 0.0) for a in mats]
    t = [eye + x for x in pw]
    for _ in range(3):
        pw = [_mm1(x, x).astype(BF16) for x in pw]
        t = [x + _mm1(x, y) for x, y in zip(t, pw)]
    for s in (4, 5, 6):
        off = [jnp.where(blk(s + 1), jnp.where(blk(s), 0.0, a), 0.0).astype(BF16) for a in mats]
        tb = [x.astype(BF16) for x in t]
        mid = [_mm1(o, y) for o, y in zip(off, tb)]
        t = [x - _mm1(y, m) for x, y, m in zip(t, tb, mid)]
    return t


def _softplus(x):
    return jnp.maximum(x, 0.0) + jnp.log(1.0 + jnp.exp(-jnp.abs(x)))


def _lane_pick(x, idx):
    lane = lax.broadcasted_iota(jnp.int32, x.shape, 1)
    return jnp.sum(jnp.where(lane == idx, x, 0.0), axis=-1, keepdims=True)


def _delta_prompt_kernel(alog_ref, dtb_ref, bq_ref, bk_ref, bv_ref, bz_ref, ba_ref, cw_ref, cb_ref, s0_ref,
                         dng_ref, o_ref, s_ref, ext_scr, st_scr):
    c = pl.program_id(1)
    group, ch = bq_ref.shape[:2]
    heads = range(B_HEADS)
    chains = [(n, h) for n in range(group) for h in heads]
    every = range(len(chains))

    @pl.when(c == 0)
    def _():
        ext_scr[:, :, 0:SUBLANES, :] = jnp.zeros((group, 3, SUBLANES, B_WIDTH), F32)
        for n in range(group):
            for i in range(3):
                ext_scr[n, i, SUBLANES - (CONV_W - 1):SUBLANES, :] = cb_ref[n, :, i * B_WIDTH:(i + 1) * B_WIDTH]
        st_scr[...] = s0_ref[...]

    conv = []
    for n in range(group):
        per_seq = []
        for i, ref in enumerate((bq_ref, bk_ref, bv_ref)):
            ext_scr[n, i, SUBLANES:SUBLANES + ch, :] = ref[n]
            acc = jnp.zeros((ch, B_WIDTH), F32)
            for j in range(CONV_W):
                w = cw_ref[j:j + 1, i * B_WIDTH:(i + 1) * B_WIDTH]
                acc = acc + ext_scr[n, i, pl.ds(SUBLANES - (CONV_W - 1) + j, ch), :] * w
            ext_scr[n, i, 0:SUBLANES, :] = ext_scr[n, i, ch:ch + SUBLANES, :]
            per_seq.append(acc * _sigmoid(acc))
        conv.append(per_seq)

    row = lax.broadcasted_iota(jnp.int32, (ch, ch), 0)
    col = lax.broadcasted_iota(jnp.int32, (ch, ch), 1)
    tri = jnp.where(row >= col, 1.0, 0.0).astype(BF16)
    dg = lambda x, y: lax.dot_general(x, y, _NN, preferred_element_type=F32)

    sl = [slice(h * B_HEAD_DIM, (h + 1) * B_HEAD_DIM) for h in heads]
    unit = lambda x: x * lax.rsqrt(jnp.sum(x * x, axis=-1, keepdims=True) + EPS)
    q = [unit(conv[n][0][:, sl[h]]) * (B_HEAD_DIM ** -0.5) for n, h in chains]
    k = [unit(conv[n][1][:, sl[h]]) for n, h in chains]
    v = [conv[n][2][:, sl[h]] for n, h in chains]
    pick = jnp.concatenate([jnp.where(row == c, 1.0, 0.0).astype(BF16) for c in range(2 * B_HEADS)], axis=1)
    spread = [sum(dg(piece, pick) for piece in _bf16_pieces(ba_ref[n])) for n in range(group)]
    beta = [_sigmoid(spread[n][:, sl[h]]) for n, h in chains]

    gc = []
    for n, h in chains:
        logit = spread[n][:, (B_HEADS + h) * LANES:(B_HEADS + h + 1) * LANES]
        g = -jnp.exp(jnp.full((1, 1), alog_ref[h], F32)) * _softplus(logit + dtb_ref[h])
        gc.append(sum(dg(tri, piece) for piece in _bf16_pieces(g)))

    decay = [jnp.exp(jnp.where(row >= col, x - x.T, NEG)) for x in gc]
    egc = [jnp.exp(x) for x in gc]
    kb = [k[i] * beta[i] for i in every]
    vb = [v[i] * beta[i] for i in every]
    akk = [_mm1(kb[i], k[i], _NT) for i in every]
    qk = [_mm1(q[i], k[i], _NT) for i in every]
    a = [jnp.where(row > col, akk[i] * decay[i], 0.0) for i in every]
    attn = [qk[i] * decay[i] for i in every]
    t = _unit_lower_inverse(a, row, col)
    u = [_mm1(t[i], vb[i]) for i in every]
    w = [_mm1(t[i], kb[i] * egc[i]) for i in every]

    gl = [x[ch - 1:ch, :] for x in gc]
    st = [st_scr[n, h] for n, h in chains]
    w_s = [_mm1(w[i], st[i]) for i in every]
    q_s = [_mm1(q[i] * egc[i], st[i]) for i in every]
    v_new = [u[i] - w_s[i] for i in every]
    a_v = [_mm1(attn[i], v_new[i]) for i in every]
    kd_t = [(k[i] * jnp.exp(gl[i] - gc[i])).T for i in every]
    k_v = [_mm1(kd_t[i], v_new[i]) for i in every]
    for i, (n, h) in enumerate(chains):
        st_scr[n, h] = st[i] * jnp.exp(gl[i]) + k_v[i]
        zz = bz_ref[n, :, sl[h]]
        o_ref[n, :, sl[h]] = (_rms(q_s[i] + a_v[i], dng_ref[...]) * (zz * _sigmoid(zz))).astype(o_ref.dtype)

    @pl.when(c == pl.num_programs(1) - 1)
    def _():
        s_ref[...] = st_scr[...]


def _delta_prompt(z3, conv_w, conv0, s0, a_log, dt_bias, dn_g, *, group, chunk):
    assert chunk == B_HEAD_DIM
    batch, seq, _ = z3.shape
    nc = seq // chunk
    smem = pl.BlockSpec(memory_space=pltpu.SMEM)
    state = (B_HEADS, B_HEAD_DIM, B_HEAD_DIM)
    zb = lambda col, w: pl.BlockSpec((group, chunk, w), lambda n, c: (n, c, col // w))
    return pl.pallas_call(
        _delta_prompt_kernel,
        grid=(batch // group, nc),
        in_specs=[
            smem, smem,
            zb(Z_BQ, B_WIDTH), zb(Z_BK, B_WIDTH), zb(Z_BV, B_WIDTH), zb(Z_BZ, B_WIDTH), zb(Z_BA, LANES),
            pl.BlockSpec(conv_w.shape, lambda n, c: (0, 0)),
            pl.BlockSpec((group, CONV_W - 1, 3 * B_WIDTH), lambda n, c: (n, 0, 0)),
            pl.BlockSpec((group,) + state, lambda n, c: (n, 0, 0, 0)),
            pl.BlockSpec((1, B_HEAD_DIM), lambda n, c: (0, 0)),
        ],
        out_specs=[
            pl.BlockSpec((group, chunk, B_WIDTH), lambda n, c: (n, c, 0)),
            pl.BlockSpec((group,) + state, lambda n, c: (n, 0, 0, 0)),
        ],
        out_shape=[
            jax.ShapeDtypeStruct((batch, seq, B_WIDTH), BF16),
            jax.ShapeDtypeStruct((batch,) + state, F32),
        ],
        scratch_shapes=[
            pltpu.VMEM((group, 3, chunk + SUBLANES, B_WIDTH), F32),
            pltpu.VMEM((group,) + state, F32),
        ],
        compiler_params=_cparams(("parallel", "arbitrary")),
        name="delta_prompt",
    )(a_log, dt_bias, z3, z3, z3, z3, z3, conv_w, conv0, s0, dn_g)


def _delta_sample_kernel(alog_ref, dtb_ref, bq_ref, bk_ref, bv_ref, bz_ref, ba_ref, cw_ref, cb_ref, s0_ref,
                         dng_ref, o_ref, s_ref, *, new_rows):
    d = B_HEAD_DIM
    group = bq_ref.shape[0]
    chains = [(n, h) for n in range(group) for h in range(B_HEADS)]

    def conv_rows(n, x_ref, i):
        x = x_ref[n]
        cb = cb_ref[n]
        seg = slice(i * B_WIDTH, (i + 1) * B_WIDTH)
        ext = [cb[r:r + 1, seg] for r in range(CONV_W - 1)] + [x[t:t + 1, :] for t in range(new_rows)]
        out = []
        for t in range(new_rows):
            acc = ext[t] * cw_ref[0:1, seg]
            for j in range(1, CONV_W):
                acc = acc + ext[t + j] * cw_ref[j:j + 1, seg]
            out.append(acc * _sigmoid(acc))
        return out

    qs = [conv_rows(n, bq_ref, 0) for n in range(group)]
    ks = [conv_rows(n, bk_ref, 1) for n in range(group)]
    vs = [conv_rows(n, bv_ref, 2) for n in range(group)]
    sl = [slice(h * d, (h + 1) * d) for h in range(B_HEADS)]
    beta = [_sigmoid(_lane_pick(ba_ref[n], h)) for n, h in chains]
    g = [-jnp.exp(jnp.full((1, 1), alog_ref[h], F32)) * _softplus(_lane_pick(ba_ref[n], B_HEADS + h) + dtb_ref[h])
         for n, h in chains]
    eye = lax.broadcasted_iota(jnp.int32, (d, d), 0) == lax.broadcasted_iota(jnp.int32, (d, d), 1)
    to_col = lambda r: jnp.sum(jnp.where(eye, r, 0.0), axis=-1, keepdims=True)
    unit = lambda r: r * lax.rsqrt(jnp.sum(r * r, axis=-1, keepdims=True) + EPS)

    st = [s0_ref[n, h] for n, h in chains]
    o_ref[...] = jnp.zeros(o_ref.shape, F32)
    for t in range(new_rows):
        for i, (n, h) in enumerate(chains):
            a = jnp.exp(g[i][t:t + 1, :])
            kc = to_col(unit(ks[n][t][:, sl[h]]))
            k_s = jnp.sum(kc * st[i], axis=0, keepdims=True)
            st[i] = a * st[i] + kc * (beta[i][t:t + 1, :] * (vs[n][t][:, sl[h]] - a * k_s))
            qc = to_col(unit(qs[n][t][:, sl[h]]) * (d ** -0.5))
            o = jnp.sum(qc * st[i], axis=0, keepdims=True)
            zz = bz_ref[n, t:t + 1, sl[h]]
            o_ref[n, t:t + 1, sl[h]] = _rms(o, dng_ref[...]) * (zz * _sigmoid(zz))
    for i, (n, h) in enumerate(chains):
        s_ref[n, h] = st[i]


def _delta_sample(z3, conv_w, conv_buf, s0, a_log, dt_bias, dn_g, *, layer, group, new_rows):
    n, rows, _ = z3.shape
    smem = pl.BlockSpec(memory_space=pltpu.SMEM)
    zb = lambda col: pl.BlockSpec((group, rows, B_WIDTH), lambda b: (b, 0, col // B_WIDTH))
    state = (B_HEADS, B_HEAD_DIM, B_HEAD_DIM)
    kern = functools.partial(_delta_sample_kernel, new_rows=new_rows)
    return dict(
        kernel=kern,
        in_specs=[
            smem, smem,
            zb(Z_BQ), zb(Z_BK), zb(Z_BV), zb(Z_BZ),
            pl.BlockSpec((group, rows, LANES), lambda b: (b, 0, Z_BA // LANES)),
            pl.BlockSpec(conv_w.shape, lambda b: (0, 0)),
            pl.BlockSpec((None, group, CONV_W - 1, 3 * B_WIDTH), lambda b: (layer, b, 0, 0)),
            pl.BlockSpec((None, group) + state, lambda b: (layer, b, 0, 0, 0)),
            pl.BlockSpec((1, B_HEAD_DIM), lambda b: (0, 0)),
        ],
        args=[a_log, dt_bias, z3, z3, z3, z3, z3, conv_w, conv_buf, s0, dn_g],
        out_specs=[
            pl.BlockSpec((group, rows, B_WIDTH), lambda b: (b, 0, 0)),
            pl.BlockSpec((group,) + state, lambda b: (b, 0, 0, 0)),
        ],
        out_shape=[
            jax.ShapeDtypeStruct((n, rows, B_WIDTH), F32),
            jax.ShapeDtypeStruct((n,) + state, F32),
        ],
    )


def _fused_kernel(*refs, parts):
    n_in = sum(p[1] for p in parts)
    i, o = 0, n_in
    for body, ni, no in parts:
        body(*refs[i:i + ni], *refs[o:o + no])
        i, o = i + ni, o + no


def _sample_mixers(parts, n):
    kern = functools.partial(
        _fused_kernel, parts=tuple((p["kernel"], len(p["in_specs"]), len(p["out_specs"])) for p in parts))
    return pl.pallas_call(
        kern,
        grid=(n,),
        in_specs=[s for p in parts for s in p["in_specs"]],
        out_specs=[s for p in parts for s in p["out_specs"]],
        out_shape=[s for p in parts for s in p["out_shape"]],
        compiler_params=_cparams(("arbitrary",)),
        name="sample_mixers",
    )(*[a for p in parts for a in p["args"]])


def _pair_order(rows):
    shape = rows.shape
    rows = rows.reshape(shape[0], 2, 2, 3, *shape[2:])
    return jnp.swapaxes(rows, 2, 3).reshape(shape)


def _layout_w_in(w_in):
    wt = jnp.swapaxes(w_in, 1, 2)
    cq = _pair_order(wt[:, O_CQ:O_CKV].reshape(DEPTH, C_HEADS, HEAD_DIM, D_MODEL)).reshape(DEPTH, C_WIDTH, D_MODEL)
    pad = lambda n: jnp.zeros((DEPTH, n, D_MODEL), w_in.dtype)
    parts = [wt[:, :O_BQKV], cq, wt[:, O_BQKV:O_BETA], pad(Z_CK - Z_BZ - B_WIDTH),
             wt[:, O_CKV:O_END], wt[:, O_BETA:O_CQ], pad(Z_COLS - Z_BA - 2 * B_HEADS)]
    return jnp.concatenate(parts, axis=1).astype(BF16)


def _layout_w_out(w_out):
    c_rows = _pair_order(w_out[:, A_WIDTH + B_WIDTH:].reshape(DEPTH, C_HEADS, HEAD_DIM, D_MODEL))
    c_rows = c_rows.reshape(DEPTH, C_WIDTH, D_MODEL)
    return jnp.concatenate([w_out[:, :A_WIDTH + B_WIDTH], c_rows], axis=1).astype(BF16)


def _pair_table(v):
    return jnp.asarray(v, F32).reshape(-1, 2)


def _feature_major(cache):
    depth, n, tokens = cache.shape[:3]
    return jnp.transpose(cache, (0, 1, 3, 4, 5, 2)).reshape(depth, n, -1, tokens)


def kernel(x_prompt, x_sample, cache_dilated_kv, cache_swa_kv, state_delta_s, state_delta_conv, g_pre_mix,
           w_in, delta_conv_w, delta_a_log, delta_dt_bias, delta_norm_g, swa_sinks, w_out, g_post_mix,
           g_pre_mlp, w_up, w_down, g_post_mlp):
    batch, seq, _ = x_prompt.shape
    dec_batch, dec_seq, _ = x_sample.shape
    past_a = cache_dilated_kv.shape[2]
    past_c = cache_swa_kv.shape[2]
    assert seq % (TQ * A_CLASS_BLOCKS) == 0 and seq == A_WINDOW_MAX and dec_seq <= SAMPLE_ROWS
    assert past_a == A_WINDOW_MAX and past_c == C_WINDOW

    w_in_p = _layout_w_in(w_in)
    w_out_p = _layout_w_out(w_out)
    w_up_b = w_up.astype(BF16)
    w_dn_b = w_down.astype(BF16)
    cache_a = _feature_major(cache_dilated_kv)
    cache_c = _feature_major(cache_swa_kv)

    tab_pa = _prompt_tables_a(seq)
    tab_pc = _prompt_tables_c()
    tab_sa = _sample_tables(_log_mult_a, past_a, SAMPLE_ROWS)
    tab_sc = _sample_tables(_log_mult_c, past_c, SAMPLE_ROWS)
    alibi_a = _alibi(A_HEADS)
    alibi_c = _alibi(C_HEADS)[np.asarray(C_HEAD_ORDER)]
    slopes_a = _pair_table(alibi_a)
    slopes_c = _pair_table(alibi_c)
    no_sinks = jnp.zeros_like(slopes_a)
    c_order = np.asarray(C_HEAD_ORDER)

    conv0 = jnp.zeros((batch, CONV_W - 1, 3 * B_WIDTH), F32)
    s_zero = jnp.zeros((batch, B_HEADS, B_HEAD_DIM, B_HEAD_DIM), F32)

    xp = x_prompt.reshape(batch * seq, D_MODEL)
    xs = jnp.pad(x_sample, ((0, 0), (0, SAMPLE_ROWS - dec_seq), (0, 0))).reshape(dec_batch * SAMPLE_ROWS, D_MODEL)
    ms = xs.shape[0]

    p_s, p_conv = [], []
    s_akv, s_ckv, s_s, s_conv = [], [], [], []
    kv_t = tuple(jnp.zeros((DEPTH, batch, w, seq), F32) for w in (2 * A_WIDTH, 2 * C_KV_WIDTH))
    for l in range(DEPTH):
        row = lambda a: a[l].reshape(1, -1)
        sinks_c = _pair_table(swa_sinks[l][c_order])
        zp, kv_t = _inproj(xp, row(g_pre_mix), w_in_p, l, 1024, kv_t=kv_t, seq=seq)
        zs, _ = _inproj(xs, row(g_pre_mix), w_in_p, l, ms)
        zs3 = zs.reshape(dec_batch, SAMPLE_ROWS, Z_COLS)

        a_p = _attn_a_prompt(zp, slopes_a, tab_pa, batch=batch, seq=seq)
        c_p = _attn_c_prompt(zp, slopes_c, sinks_c, tab_pc, batch=batch, seq=seq, blocks=16)
        zp3 = zp.reshape(batch, seq, Z_COLS)
        b_p, st_p = _delta_prompt(zp3, delta_conv_w[l], conv0, s_zero, delta_a_log[l], delta_dt_bias[l],
                                  row(delta_norm_g), group=batch, chunk=B_HEAD_DIM)
        b_p = b_p.reshape(batch * seq, B_WIDTH)

        a_s = _attn_sample(zs3, cache_a, alibi_a, no_sinks, tab_sa, layer=l, group=1, q_col=Z_AQ, k_col=Z_AK,
                           v_col=Z_AV, kv_width=A_WIDTH, kv_of_pair=lambda p: p, new_rows=dec_seq,
                           has_sink=False, name="attn_a_sample")
        c_s = _attn_sample(zs3, cache_c, alibi_c, sinks_c, tab_sc, layer=l, group=1, q_col=Z_CQ, k_col=Z_CK,
                           v_col=Z_CV, kv_width=C_KV_WIDTH, kv_of_pair=lambda p: p // 3, new_rows=dec_seq,
                           has_sink=True, name="attn_c_sample")
        b_s = _delta_sample(zs3, delta_conv_w[l], state_delta_conv, state_delta_s, delta_a_log[l],
                            delta_dt_bias[l], row(delta_norm_g), layer=l, group=1, new_rows=dec_seq)
        a_s, c_s, b_s, st_s = _sample_mixers([a_s, c_s, b_s], dec_batch)

        mlp_w = (w_out_p, row(g_post_mix), row(g_pre_mlp), w_up_b, w_dn_b, row(g_post_mlp))
        xp = _mlp(a_p, b_p, c_p, xp, *mlp_w, l, 512, 512)
        xs = _mlp(a_s.reshape(ms, A_WIDTH), b_s.reshape(ms, B_WIDTH), c_s.reshape(ms, C_WIDTH), xs,
                  *mlp_w, l, ms, 1024)

        p_s.append(st_p)
        p_conv.append(zp3[:, seq - (CONV_W - 1):, Z_BQ:Z_BQ + 3 * B_WIDTH])
        s_akv.append(zs3[:, :dec_seq, Z_AK:Z_AK + 2 * A_WIDTH].reshape(dec_batch, dec_seq, 2, A_HEADS, HEAD_DIM))
        s_ckv.append(zs3[:, :dec_seq, Z_CK:Z_CK + 2 * C_KV_WIDTH]
                     .reshape(dec_batch, dec_seq, 2, C_KV_HEADS, HEAD_DIM))
        s_s.append(st_s)
        full = jnp.concatenate([state_delta_conv[l], zs3[:, :dec_seq, Z_BQ:Z_BQ + 3 * B_WIDTH]], axis=1)
        s_conv.append(full[:, full.shape[1] - (CONV_W - 1):])

    yp = xp.reshape(batch, seq, D_MODEL)
    ys = xs.reshape(dec_batch, SAMPLE_ROWS, D_MODEL)[:, :dec_seq]
    token_major = lambda t, heads: jnp.transpose(
        t.reshape(DEPTH, batch, 2, heads, HEAD_DIM, t.shape[-1]), (0, 1, 5, 2, 3, 4))
    p_akv = token_major(kv_t[0][..., seq - A_WINDOW_MAX:], A_HEADS)
    p_ckv = token_major(kv_t[1][..., seq - C_WINDOW:], C_KV_HEADS)
    return (yp, ys, p_akv, p_ckv, jnp.stack(p_s), jnp.stack(p_conv),
            jnp.stack(s_akv), jnp.stack(s_ckv), jnp.stack(s_s), jnp.stack(s_conv))
```
